```python
import math
import jax, jax.numpy as jnp
from jax import lax
import numpy as np

D_MODEL = 1024
BATCH = 8
SEQ = 4096
DEPTH = 2
DEC_BATCH = 32
DEC_SEQ = 4
PAST_LEN = 16384
PAGE_SIZE = 128

SB_HEADS = 8
SB_HEAD_DIM = 64
SB_WIDTH = SB_HEADS * SB_HEAD_DIM
Q_BLOCK = 128
RG_WIDTH = 512
RG_BLOCKS = 8
RG_BLOCK_DIM = RG_WIDTH // RG_BLOCKS
RG_C = 8.0
CONV_WIDTH = 4
SSD_INNER = 512
SSD_HEAD_DIM = 64
SSD_HEADS = SSD_INNER // SSD_HEAD_DIM
SSD_GROUPS = 2
SSD_STATE = 128
SSD_CHUNK = 128
SSD_CONV_DIM = SSD_INNER + 2 * SSD_GROUPS * SSD_STATE
N_BRANCH = 3
MIX_WIDTH = SB_WIDTH + RG_WIDTH + SSD_INNER
IN_SIZES = (SB_WIDTH, SB_WIDTH, SB_WIDTH, RG_WIDTH, RG_WIDTH, SSD_INNER, SSD_CONV_DIM, SSD_HEADS, N_BRANCH * D_MODEL)
IN_COLS = 3 * SB_WIDTH + 2 * RG_WIDTH + SSD_INNER + SSD_CONV_DIM + SSD_HEADS + N_BRANCH * D_MODEL
N_EXPERTS = 32
TOP_K = 4
D_FF = D_MODEL
SWIGLU_LIMIT = 7.0
SWIGLU_ALPHA = 1.702
MOE_BLOCK = 128
EPS = 1e-6
F32 = jnp.float32

kernel_name = 'hybrid_sb_rglru_ssd_moe_step'


def split_cols(a, sizes):
    idx = [int(i) for i in np.cumsum(sizes)[:-1]]
    return jnp.split(a, idx, axis=-1)


def rmsnorm(x, g):
    xf = x.astype(F32)
    return xf * lax.rsqrt(jnp.mean(xf * xf, axis=-1, keepdims=True) + EPS) * g.astype(F32)


def causal_conv(x, prev, w, b):
    T = x.shape[1]
    xp = jnp.concatenate([prev.astype(x.dtype), x], axis=1)
    y = b
    for j in range(w.shape[0]):
        y = y + xp[:, j:j + T] * w[j]
    return y, xp[:, T:]


def stick_breaking(z, mask):
    log_keep = jnp.where(mask, jax.nn.log_sigmoid(-z), 0.0)
    after = lax.cumsum(log_keep, axis=z.ndim - 1, reverse=True) - log_keep
    return jnp.where(mask, jnp.exp(jax.nn.log_sigmoid(z) + after), 0.0)


def sb_prompt(q, k, v, bias):
    B, S, H, Dh = q.shape
    nb = S // Q_BLOCK
    scale = Dh ** -0.5
    qb = jnp.moveaxis(q.reshape(B, nb, Q_BLOCK, H, Dh), 1, 0)
    k_pos = jnp.arange(S)
    vf = v.astype(F32)
    bh = bias.astype(F32)[:, None, None]

    def block(args):
        qi, i = args
        z = jnp.einsum('bqhd,bkhd->bhqk', qi, k).astype(F32) * scale + bh
        q_pos = i * Q_BLOCK + jnp.arange(Q_BLOCK)
        w = stick_breaking(z, k_pos[None, :] < q_pos[:, None])
        return jnp.einsum('bhqk,bkhd->bqhd', w, vf)

    out = lax.map(block, (qb, jnp.arange(nb)))
    return jnp.moveaxis(out, 0, 1).reshape(B, S, H * Dh)


def sb_sample(q, k, v, k_past, v_past, bias):
    B, T, H, Dh = q.shape
    P = k_past.shape[1]
    scale = Dh ** -0.5
    z = jnp.concatenate([jnp.einsum('bqhd,bkhd->bhqk', q, k_past),
                         jnp.einsum('bqhd,bkhd->bhqk', q, k)], axis=-1).astype(F32) * scale
    z = z + bias.astype(F32)[:, None, None]
    mask = jnp.concatenate([jnp.ones((T, P), bool), jnp.arange(T)[None, :] < jnp.arange(T)[:, None]], axis=1)
    w = stick_breaking(z, mask)
    out = (jnp.einsum('bhqk,bkhd->bqhd', w[..., :P], v_past.astype(F32))
           + jnp.einsum('bhqk,bkhd->bqhd', w[..., P:], v.astype(F32)))
    return out.reshape(B, T, H * Dh)


def rg_lru(x, h0, start, w_a, b_a, w_x, b_x, lam):
    B, T, W = x.shape
    xb = x.reshape(B, T, RG_BLOCKS, RG_BLOCK_DIM)
    r = jax.nn.sigmoid(jnp.einsum('btni,nij->btnj', xb, w_a).reshape(B, T, W) + b_a)
    i = jax.nn.sigmoid(jnp.einsum('btni,nij->btnj', xb, w_x).reshape(B, T, W) + b_x)
    log_a = (-RG_C * r * jax.nn.softplus(-lam)).astype(F32)
    a = jnp.exp(log_a)
    first = (start + jnp.arange(T)) == 0
    mult = jnp.where(first[None, :, None], 1.0, jnp.sqrt(-jnp.expm1(2.0 * log_a)))
    u = (x * i).astype(F32) * mult

    def step(h, au):
        a_t, u_t = au
        h = a_t * h + u_t
        return h, h

    hT, hs = lax.scan(step, h0.astype(F32), (jnp.moveaxis(a, 1, 0), jnp.moveaxis(u, 1, 0)))
    return jnp.moveaxis(hs, 0, 1), hT


def ssd_scan(x, dt, A, Bm, Cm, h0):
    Bsz, T, H, P = x.shape
    G, N = Bm.shape[2], Bm.shape[3]
    R = H // G
    L = SSD_CHUNK if T % SSD_CHUNK == 0 else T
    nc = T // L
    xc = x.astype(F32).reshape(Bsz, nc, L, G, R, P)
    dtc = dt.astype(F32).reshape(Bsz, nc, L, G, R)
    bc = Bm.astype(F32).reshape(Bsz, nc, L, G, N)
    cc = Cm.astype(F32).reshape(Bsz, nc, L, G, N)
    cs = jnp.cumsum(dtc * A.reshape(G, R), axis=2)
    seg = cs[:, :, :, None] - cs[:, :, None, :]
    tril = jnp.tril(jnp.ones((L, L), bool))[:, :, None, None]
    decay = jnp.exp(jnp.where(tril, seg, -jnp.inf))
    cb = jnp.einsum('bclgn,bcsgn->bclsg', cc, bc)
    wts = cb[..., None] * decay * dtc[:, :, None]
    y_diag = jnp.einsum('bclsgr,bcsgrp->bclgrp', wts, xc)
    decay_end = jnp.exp(cs[:, :, -1:] - cs)
    states = jnp.einsum('bclgn,bclgr,bclgrp->bcgrpn', bc, decay_end * dtc, xc)
    chunk_decay = jnp.exp(cs[:, :, -1])

    def step(h, inp):
        cd, st = inp
        return cd[..., None, None] * h + st, h

    hT, h_in = lax.scan(step, h0.astype(F32).reshape(Bsz, G, R, P, N),
                        (jnp.moveaxis(chunk_decay, 1, 0), jnp.moveaxis(states, 1, 0)))
    h_in = jnp.moveaxis(h_in, 0, 1)
    y_off = jnp.einsum('bclgn,bcgrpn->bclgrp', cc, h_in) * jnp.exp(cs)[..., None]
    y = (y_diag + y_off).reshape(Bsz, T, H, P)
    return y, hT.reshape(Bsz, H, P, N)


def gated_rmsnorm(y, z, g):
    B, T, W = y.shape
    u = (y * jax.nn.silu(z.astype(F32))).reshape(B, T, SSD_GROUPS, W // SSD_GROUPS)
    u = u * lax.rsqrt(jnp.mean(u * u, axis=-1, keepdims=True) + EPS)
    return u.reshape(B, T, W) * g


def moe(h, w_router, b_router, w_up, b_up, w_down, b_down):
    B, T, D = h.shape
    hf = h.reshape(B * T, D)
    n_tok = B * T
    n_slot = n_tok * TOP_K
    rows = min(MOE_BLOCK, max(8, n_slot // N_EXPERTS))
    n_blocks = -(-n_slot // rows) + N_EXPERTS
    logits = (hf @ w_router + b_router).astype(F32)
    top_val, top_idx = lax.top_k(logits, TOP_K)
    gate = jax.nn.softmax(top_val, axis=-1)
    e_flat = top_idx.reshape(n_slot).astype(jnp.int32)
    tok = jnp.arange(n_slot, dtype=jnp.int32) // TOP_K
    order = jnp.argsort(e_flat, stable=True)
    e_sorted = e_flat[order]
    counts = jnp.bincount(e_flat, length=N_EXPERTS)
    padded = (counts + rows - 1) // rows * rows
    start = jnp.cumsum(counts) - counts
    pend = jnp.cumsum(padded)
    dest = (pend - padded)[e_sorted] + jnp.arange(n_slot) - start[e_sorted]
    row_tok = jnp.full((n_blocks * rows,), n_tok, jnp.int32).at[dest].set(tok[order])
    row_gate = jnp.zeros((n_blocks * rows,), F32).at[dest].set(gate.reshape(n_slot)[order])
    block_exp = jnp.minimum(jnp.searchsorted(pend, jnp.arange(n_blocks) * rows, side='right'), N_EXPERTS - 1)
    xs = jnp.concatenate([hf, jnp.zeros((1, D), hf.dtype)], axis=0)[row_tok].reshape(n_blocks, rows, D)

    def expert_block(args):
        xb, e = args
        u = xb @ w_up[e] + b_up[e]
        glu = jnp.minimum(u[:, 0::2], SWIGLU_LIMIT)
        lin = jnp.clip(u[:, 1::2], -SWIGLU_LIMIT, SWIGLU_LIMIT)
        act = glu * jax.nn.sigmoid(SWIGLU_ALPHA * glu) * (lin + 1.0)
        return act @ w_down[e] + b_down[e]

    ys = lax.map(expert_block, (xs, block_exp)).reshape(n_blocks * rows, D)
    out = jnp.zeros((n_tok + 1, D), ys.dtype).at[row_tok].add(ys * row_gate[:, None].astype(ys.dtype))
    return out[:n_tok].reshape(B, T, D)


def layer(x, c, l, p, attend, rg_conv0, rg_h0, ssd_conv0, ssd_h0, start):
    B, T, _ = x.shape
    mod = jax.nn.silu(c.astype(F32)) @ p['w_mod'][l] + p['b_mod'][l]
    sh1, sc1, gt1, sh2, sc2, gt2 = jnp.split(mod[:, None, :], 6, axis=-1)
    h = rmsnorm(x, p['norm_mix'][l]) * (1.0 + sc1) + sh1
    proj = h @ p['w_in'][l]
    q, k, v, rg_x, rg_g, z, xbc, dt_raw, br_g = split_cols(proj, IN_SIZES)
    q = q.reshape(B, T, SB_HEADS, SB_HEAD_DIM)
    k = k.reshape(B, T, SB_HEADS, SB_HEAD_DIM)
    v = v.reshape(B, T, SB_HEADS, SB_HEAD_DIM)
    y_a = attend(l, q, k, v, p['sb_bias'][l])
    rg_xc, rg_conv_new = causal_conv(rg_x, rg_conv0, p['rg_conv_w'][l], p['rg_conv_b'][l])
    lru, rg_h_new = rg_lru(rg_xc, rg_h0, start, p['rg_w_a'][l], p['rg_b_a'][l],
                           p['rg_w_x'][l], p['rg_b_x'][l], p['rg_lam'][l])
    y_b = jax.nn.gelu(rg_g.astype(F32)) * lru
    xbc_c, ssd_conv_new = causal_conv(xbc, ssd_conv0, p['ssd_conv_w'][l], p['ssd_conv_b'][l])
    xbc_c = jax.nn.silu(xbc_c)
    xs, bm, cm = split_cols(xbc_c, (SSD_INNER, SSD_GROUPS * SSD_STATE, SSD_GROUPS * SSD_STATE))
    xs = xs.reshape(B, T, SSD_HEADS, SSD_HEAD_DIM)
    dt = jax.nn.softplus(dt_raw.astype(F32) + p['ssd_dt_bias'][l])
    a = -jnp.exp(p['ssd_a_log'][l].astype(F32))
    y, ssd_new = ssd_scan(xs, dt, a, bm.reshape(B, T, SSD_GROUPS, SSD_STATE),
                          cm.reshape(B, T, SSD_GROUPS, SSD_STATE), ssd_h0)
    y = y + p['ssd_d'][l][:, None] * xs
    y_c = gated_rmsnorm(y.reshape(B, T, SSD_INNER), z, p['ssd_norm'][l])
    g = jax.nn.sigmoid(br_g.astype(F32)).reshape(B, T, N_BRANCH, D_MODEL)
    wb = p['w_branch'][l]
    merged = (g[:, :, 0] * (y_a @ wb[:SB_WIDTH])
              + g[:, :, 1] * (y_b @ wb[SB_WIDTH:SB_WIDTH + RG_WIDTH])
              + g[:, :, 2] * (y_c @ wb[SB_WIDTH + RG_WIDTH:]))
    x = x + (gt1 * (merged @ p['w_out'][l])).astype(x.dtype)
    h2 = rmsnorm(x, p['norm_ffn'][l]) * (1.0 + sc2) + sh2
    f = moe(h2, p['w_router'][l], p['b_router'][l], p['w_up'][l], p['b_up'][l], p['w_down'][l], p['b_down'][l])
    x = x + (gt2 * f).astype(x.dtype)
    return x, (k, v, rg_conv_new, rg_h_new, ssd_conv_new, ssd_new)


def trunk(x, c, p, attend, rg_conv0, rg_h0, ssd_conv0, ssd_h0, start):
    new = []
    for l in range(DEPTH):
        x, st = layer(x, c, l, p, attend, rg_conv0[l], rg_h0[l], ssd_conv0[l], ssd_h0[l], start)
        new.append(st)
    stacked = [jnp.stack([s[i] for s in new]) for i in range(6)]
    return rmsnorm(x, p['norm_final']).astype(x.dtype), stacked


def setup_inputs(seed: int = 0) -> dict:
    key = jax.random.key(seed)
    keys = iter(jax.random.split(key, 48))

    def nrm(shape, scale=1.0):
        return scale * jax.random.normal(next(keys), shape, F32)

    def unif(shape, lo, hi):
        return jax.random.uniform(next(keys), shape, F32, lo, hi)

    n_pages = PAST_LEN // PAGE_SIZE
    n_used = DEC_BATCH * n_pages
    n_phys = n_used + max(1, n_used // 4)
    x_prompt = nrm((BATCH, SEQ, D_MODEL))
    x_sample = nrm((DEC_BATCH, DEC_SEQ, D_MODEL))
    cache_k = nrm((DEPTH, n_phys, PAGE_SIZE, SB_HEADS, SB_HEAD_DIM))
    cache_v = nrm((DEPTH, n_phys, PAGE_SIZE, SB_HEADS, SB_HEAD_DIM))
    state_rg_conv = nrm((DEPTH, DEC_BATCH, CONV_WIDTH - 1, RG_WIDTH))
    state_rg_h = nrm((DEPTH, DEC_BATCH, RG_WIDTH))
    state_ssd_conv = nrm((DEPTH, DEC_BATCH, CONV_WIDTH - 1, SSD_CONV_DIM))
    state_ssd = nrm((DEPTH, DEC_BATCH, SSD_HEADS, SSD_HEAD_DIM, SSD_STATE), 0.5)
    page_table = jax.random.permutation(next(keys), n_phys)[:n_used].reshape(DEC_BATCH, n_pages).astype(jnp.int32)
    c_prompt = nrm((BATCH, D_MODEL))
    c_sample = nrm((DEC_BATCH, D_MODEL))
    w_mod = nrm((DEPTH, D_MODEL, 6 * D_MODEL), 0.5 * D_MODEL ** -0.5)
    b_mod = nrm((DEPTH, 6 * D_MODEL), 0.01)
    norm_mix = 1.0 + nrm((DEPTH, D_MODEL), 0.02)
    norm_ffn = 1.0 + nrm((DEPTH, D_MODEL), 0.02)
    norm_final = 1.0 + nrm((D_MODEL,), 0.02)
    w_in = nrm((DEPTH, D_MODEL, IN_COLS), D_MODEL ** -0.5)
    sb_bias = -math.log(PAST_LEN) + nrm((DEPTH, SB_HEADS), 0.3)
    rg_conv_w = nrm((DEPTH, CONV_WIDTH, RG_WIDTH), CONV_WIDTH ** -0.5)
    rg_conv_b = nrm((DEPTH, RG_WIDTH), 0.01)
    rg_w_a = nrm((DEPTH, RG_BLOCKS, RG_BLOCK_DIM, RG_BLOCK_DIM), RG_BLOCK_DIM ** -0.5)
    rg_b_a = nrm((DEPTH, RG_WIDTH), 0.01)
    rg_w_x = nrm((DEPTH, RG_BLOCKS, RG_BLOCK_DIM, RG_BLOCK_DIM), RG_BLOCK_DIM ** -0.5)
    rg_b_x = nrm((DEPTH, RG_WIDTH), 0.01)
    u = unif((DEPTH, RG_WIDTH), 0.9, 0.999)
    rg_lam = jnp.log(u) - jnp.log1p(-u)
    ssd_conv_w = nrm((DEPTH, CONV_WIDTH, SSD_CONV_DIM), CONV_WIDTH ** -0.5)
    ssd_conv_b = nrm((DEPTH, SSD_CONV_DIM), 0.01)
    dt0 = jnp.exp(unif((DEPTH, SSD_HEADS), math.log(1e-3), math.log(1e-1)))
    ssd_dt_bias = dt0 + jnp.log(-jnp.expm1(-dt0))
    ssd_a_log = jnp.log(unif((DEPTH, SSD_HEADS), 1.0, 16.0))
    ssd_d = 1.0 + nrm((DEPTH, SSD_HEADS), 0.1)
    ssd_norm = 1.0 + nrm((DEPTH, SSD_INNER), 0.02)
    w_branch = nrm((DEPTH, MIX_WIDTH, D_MODEL), SB_WIDTH ** -0.5)
    w_out = nrm((DEPTH, D_MODEL, D_MODEL), D_MODEL ** -0.5)
    w_router = nrm((DEPTH, D_MODEL, N_EXPERTS), D_MODEL ** -0.5)
    b_router = nrm((DEPTH, N_EXPERTS), 0.01)
    w_up = nrm((DEPTH, N_EXPERTS, D_MODEL, 2 * D_FF), D_MODEL ** -0.5)
    b_up = nrm((DEPTH, N_EXPERTS, 2 * D_FF), 0.01)
    w_down = nrm((DEPTH, N_EXPERTS, D_FF, D_MODEL), D_FF ** -0.5)
    b_down = nrm((DEPTH, N_EXPERTS, D_MODEL), 0.01)
    return {'x_prompt': x_prompt, 'x_sample': x_sample, 'cache_k': cache_k, 'cache_v': cache_v,
            'state_rg_conv': state_rg_conv, 'state_rg_h': state_rg_h, 'state_ssd_conv': state_ssd_conv,
            'state_ssd': state_ssd, 'page_table': page_table, 'c_prompt': c_prompt, 'c_sample': c_sample,
            'w_mod': w_mod, 'b_mod': b_mod, 'norm_mix': norm_mix, 'norm_ffn': norm_ffn, 'norm_final': norm_final,
            'w_in': w_in, 'sb_bias': sb_bias, 'rg_conv_w': rg_conv_w, 'rg_conv_b': rg_conv_b, 'rg_w_a': rg_w_a,
            'rg_b_a': rg_b_a, 'rg_w_x': rg_w_x, 'rg_b_x': rg_b_x, 'rg_lam': rg_lam, 'ssd_conv_w': ssd_conv_w,
            'ssd_conv_b': ssd_conv_b, 'ssd_dt_bias': ssd_dt_bias, 'ssd_a_log': ssd_a_log, 'ssd_d': ssd_d,
            'ssd_norm': ssd_norm, 'w_branch': w_branch, 'w_out': w_out, 'w_router': w_router, 'b_router': b_router,
            'w_up': w_up, 'b_up': b_up, 'w_down': w_down, 'b_down': b_down}


def reference(x_prompt, x_sample, cache_k, cache_v, state_rg_conv, state_rg_h, state_ssd_conv, state_ssd,
              page_table, c_prompt, c_sample, w_mod, b_mod, norm_mix, norm_ffn, norm_final, w_in, sb_bias,
              rg_conv_w, rg_conv_b, rg_w_a, rg_b_a, rg_w_x, rg_b_x, rg_lam, ssd_conv_w, ssd_conv_b,
              ssd_dt_bias, ssd_a_log, ssd_d, ssd_norm, w_branch, w_out, w_router, b_router,
              w_up, b_up, w_down, b_down):
    p = {'w_mod': w_mod, 'b_mod': b_mod, 'norm_mix': norm_mix, 'norm_ffn': norm_ffn, 'norm_final': norm_final,
         'w_in': w_in, 'sb_bias': sb_bias, 'rg_conv_w': rg_conv_w, 'rg_conv_b': rg_conv_b, 'rg_w_a': rg_w_a,
         'rg_b_a': rg_b_a, 'rg_w_x': rg_w_x, 'rg_b_x': rg_b_x, 'rg_lam': rg_lam, 'ssd_conv_w': ssd_conv_w,
         'ssd_conv_b': ssd_conv_b, 'ssd_dt_bias': ssd_dt_bias, 'ssd_a_log': ssd_a_log, 'ssd_d': ssd_d,
         'ssd_norm': ssd_norm, 'w_branch': w_branch, 'w_out': w_out, 'w_router': w_router, 'b_router': b_router,
         'w_up': w_up, 'b_up': b_up, 'w_down': w_down, 'b_down': b_down}

    bp = x_prompt.shape[0]
    dt_p = x_prompt.dtype
    rg_conv0 = jnp.zeros((DEPTH, bp, CONV_WIDTH - 1, RG_WIDTH), dt_p)
    rg_h0 = jnp.zeros((DEPTH, bp, RG_WIDTH), F32)
    ssd_conv0 = jnp.zeros((DEPTH, bp, CONV_WIDTH - 1, SSD_CONV_DIM), dt_p)
    ssd_h0 = jnp.zeros((DEPTH, bp, SSD_HEADS, SSD_HEAD_DIM, SSD_STATE), F32)

    def attend_prompt(l, q, k, v, bias):
        return sb_prompt(q, k, v, bias)

    y_prompt, (k_p, v_p, rgc_p, rgh_p, ssc_p, ss_p) = trunk(
        x_prompt, c_prompt, p, attend_prompt, rg_conv0, rg_h0, ssd_conv0, ssd_h0, 0)

    db, n_pages = page_table.shape
    page = cache_k.shape[2]
    past_len = n_pages * page

    def attend_sample(l, q, k, v, bias):
        k_past = cache_k[l][page_table].reshape(db, past_len, SB_HEADS, SB_HEAD_DIM)
        v_past = cache_v[l][page_table].reshape(db, past_len, SB_HEADS, SB_HEAD_DIM)
        return sb_sample(q, k, v, k_past, v_past, bias)

    y_sample, (k_s, v_s, rgc_s, rgh_s, ssc_s, ss_s) = trunk(
        x_sample, c_sample, p, attend_sample, state_rg_conv, state_rg_h, state_ssd_conv, state_ssd, past_len)

    return (y_prompt, y_sample, k_p, v_p, rgc_p, rgh_p, ssc_p, ss_p, k_s, v_s, rgc_s, rgh_s, ssc_s, ss_s)
```

```python
import functools
import math

import jax
import jax.numpy as jnp
from jax import lax
from jax.experimental import pallas as pl
from jax.experimental.pallas import tpu as pltpu

F32 = jnp.float32
BF16 = jnp.bfloat16
HIGHEST = lax.Precision.HIGHEST

D_MODEL = 1024
SB_HEADS = 8
SB_HEAD_DIM = 64
SB_WIDTH = SB_HEADS * SB_HEAD_DIM
RG_WIDTH = 512
RG_BLOCKS = 8
RG_C = 8.0
CONV_WIDTH = 4
SSD_INNER = 512
SSD_HEAD_DIM = 64
SSD_HEADS = SSD_INNER // SSD_HEAD_DIM
SSD_GROUPS = 2
SSD_STATE = 128
SSD_CHUNK = 128
SSD_CONV_DIM = SSD_INNER + 2 * SSD_GROUPS * SSD_STATE
N_BRANCH = 3
N_EXPERTS = 32
TOP_K = 4
D_FF = D_MODEL
SWIGLU_LIMIT = 7.0
SWIGLU_ALPHA = 1.702
EPS = 1e-6

SUBLANES = 8
LANES = 128
VMEM_LIMIT = 48 * 1024 * 1024

C_Q, C_K, C_V, C_RGX, C_RGG, C_Z, C_XBC, C_BRG, C_DT = 0, 512, 1024, 1536, 2048, 2560, 3072, 4096, 7168
NP_COLS = 7680
IN_TN = 1536
ROW_PAD = 8
KEY_TILE = 128
MOE_BM = 256
PAGES_PER_STEP = 8


def _cparams(sem, vmem=VMEM_LIMIT):
    return pltpu.CompilerParams(dimension_semantics=sem, vmem_limit_bytes=vmem)


def _sigmoid(x):
    return 1.0 / (1.0 + jnp.exp(-x))


def _rms(x):
    return x * lax.rsqrt(jnp.mean(x * x, axis=-1, keepdims=True) + EPS)


def _mod_kernel(c_ref, w_ref, b_ref, o_ref):
    c = c_ref[...]
    s = c * _sigmoid(c)
    o_ref[...] = jnp.dot(s, w_ref[...], precision=HIGHEST, preferred_element_type=F32) + b_ref[...]


def _mod_call(c_all, w_mod, b_mod):
    depth, d, n = w_mod.shape
    nc = c_all.shape[0]
    tn = 1536
    return pl.pallas_call(
        _mod_kernel,
        grid=(depth, n // tn),
        in_specs=[pl.BlockSpec((nc, d), lambda l, j: (0, 0)),
                  pl.BlockSpec((None, d, tn), lambda l, j: (l, 0, j)),
                  pl.BlockSpec((None, 1, tn), lambda l, j: (l, 0, j))],
        out_specs=pl.BlockSpec((None, nc, tn), lambda l, j: (l, 0, j)),
        out_shape=jax.ShapeDtypeStruct((depth, nc, n), F32),
        compiler_params=_cparams(("arbitrary", "arbitrary")),
        name="mod",
    )(c_all, w_mod, b_mod.reshape(depth, 1, n))


def _inproj_kernel(x_ref, sc_ref, sh_ref, g_ref, w_ref, o_ref, h_ref):
    @pl.when(pl.program_id(1) == 0)
    def _():
        h = _rms(x_ref[...]) * g_ref[...]
        h_ref[...] = (h * (1.0 + sc_ref[...]) + sh_ref[...]).astype(BF16)

    o_ref[...] = jnp.dot(h_ref[...], w_ref[...], preferred_element_type=F32)


def _mod_spec(per_row, tm, rows_per_group):
    if per_row:
        return pl.BlockSpec((tm, D_MODEL), lambda i, *_: (i, 0))
    return pl.BlockSpec((None, 1, D_MODEL), lambda i, *_: ((i * tm) // rows_per_group, 0, 0))


def _inproj_call(x, sc, sh, g, w, *, per_row, rows_per_group, tm):
    r = x.shape[0]
    assert r % tm == 0 and (per_row or rows_per_group % tm == 0)
    ms = _mod_spec(per_row, tm, rows_per_group)
    return pl.pallas_call(
        _inproj_kernel,
        grid=(r // tm, NP_COLS // IN_TN),
        in_specs=[pl.BlockSpec((tm, D_MODEL), lambda i, j: (i, 0)), ms, ms,
                  pl.BlockSpec((1, D_MODEL), lambda i, j: (0, 0)),
                  pl.BlockSpec((D_MODEL, IN_TN), lambda i, j: (0, j))],
        out_specs=pl.BlockSpec((tm, IN_TN), lambda i, j: (i, j)),
        out_shape=jax.ShapeDtypeStruct((r, NP_COLS), F32),
        scratch_shapes=[pltpu.VMEM((tm, D_MODEL), BF16)],
        compiler_params=_cparams(("arbitrary", "arbitrary")),
        name="inproj",
    )(x, sc, sh, g, w)


def _sb_logs(z):
    sp = jnp.log1p(jnp.exp(-jnp.abs(z)))
    return jnp.minimum(z, 0.0) - sp, -(jnp.maximum(z, 0.0) + sp)


def _sb_tile(z, tri, c_old, mask):
    t = z.shape[1]
    ls, lk = _sb_logs(z)
    if mask is not None:
        lk = jnp.where(mask, lk, 0.0)
    hi = lk.astype(BF16)
    lo = (lk - hi.astype(F32)).astype(BF16)
    r = jnp.dot(hi, tri, preferred_element_type=F32) + jnp.dot(lo, tri, preferred_element_type=F32)
    w = jnp.exp(ls + r[:, :t] + c_old)
    if mask is not None:
        w = jnp.where(mask, w, 0.0)
    return w, c_old + r[:, t:]


def _tri_matrix(t):
    j = lax.broadcasted_iota(jnp.int32, (t, 2 * t), 0)
    s = lax.broadcasted_iota(jnp.int32, (t, 2 * t), 1)
    return jnp.where((j > s) | (s >= t), 1.0, 0.0).astype(BF16)


def _sbp_kernel(bias_ref, q_ref, k_ref, v_ref, tri_ref, o_ref, kb, vb, acc, cc, *, scale):
    t = KEY_TILE
    hp = pl.program_id(1)
    i = pl.program_id(2)

    @pl.when(i == 0)
    def _():
        kb[...] = k_ref[...].astype(BF16)
        vb[...] = v_ref[...].astype(BF16)

    lane = lax.broadcasted_iota(jnp.int32, (t, LANES), 1)
    first_head = lane < SB_HEAD_DIM
    q = q_ref[...] * scale
    qm = (jnp.where(first_head, q, 0.0).astype(BF16), jnp.where(first_head, 0.0, q).astype(BF16))
    tri = tri_ref[...]
    row = lax.broadcasted_iota(jnp.int32, (t, t), 0)
    col = lax.broadcasted_iota(jnp.int32, (t, t), 1)
    causal = col < row

    def tile(j, mask):
        start = pl.multiple_of(j * t, t)
        kt = kb[pl.ds(start, t), :]
        vt = vb[pl.ds(start, t), :]
        for hh in range(2):
            z = lax.dot_general(qm[hh], kt, (((1,), (1,)), ((), ())), preferred_element_type=F32)
            z = z + bias_ref[2 * hp + hh]
            w, c_new = _sb_tile(z, tri, cc[hh], mask)
            cc[hh] = c_new
            acc[hh] += jnp.dot(w.astype(BF16), vt, preferred_element_type=F32)

    cc[...] = jnp.zeros_like(cc)
    acc[...] = jnp.zeros_like(acc)
    tile(i, causal)

    def body(jj, carry):
        tile(i - 1 - jj, None)
        return carry

    lax.fori_loop(0, i, body, 0)
    o_ref[...] = jnp.where(first_head, acc[0], acc[1]).astype(o_ref.dtype)


def _sbp_call(proj, bias, tri):
    b, s, _ = proj.shape
    t = KEY_TILE
    assert s % t == 0
    kern = functools.partial(_sbp_kernel, scale=SB_HEAD_DIM ** -0.5)
    return pl.pallas_call(
        kern,
        grid=(b, SB_WIDTH // LANES, s // t),
        in_specs=[pl.BlockSpec(memory_space=pltpu.SMEM),
                  pl.BlockSpec((None, t, LANES), lambda bi, hp, i: (bi, i, C_Q // LANES + hp)),
                  pl.BlockSpec((None, s, LANES), lambda bi, hp, i: (bi, 0, C_K // LANES + hp)),
                  pl.BlockSpec((None, s, LANES), lambda bi, hp, i: (bi, 0, C_V // LANES + hp)),
                  pl.BlockSpec((t, 2 * t), lambda bi, hp, i: (0, 0))],
        out_specs=pl.BlockSpec((None, t, LANES), lambda bi, hp, i: (bi, i, hp)),
        out_shape=jax.ShapeDtypeStruct((b, s, SB_WIDTH), BF16),
        scratch_shapes=[pltpu.VMEM((s, LANES), BF16), pltpu.VMEM((s, LANES), BF16),
                        pltpu.VMEM((2, t, LANES), F32), pltpu.VMEM((2, t, LANES), F32)],
        compiler_params=_cparams(("arbitrary", "arbitrary", "arbitrary")),
        name="sb_prompt",
    )(bias, proj, proj, proj, tri)


def _sbs_kernel(pt_ref, qm_ref, bias_ref, kn_ref, vn_ref, tri_ref, *rest, n_pg, n_q):
    del pt_ref
    k_refs = rest[:n_pg]
    v_refs = rest[n_pg:2 * n_pg]
    o_ref, acc, cc = rest[2 * n_pg:]
    t = KEY_TILE
    m = n_q * SB_HEADS
    p = pl.program_id(1)
    qm = qm_ref[...]
    bias = bias_ref[...]
    tri = tri_ref[...]

    def tile(kt, vt, mask):
        z = jnp.dot(qm, kt, preferred_element_type=F32) + bias
        w, c_new = _sb_tile(z, tri, cc[...], mask)
        cc[...] = c_new
        acc[...] += lax.dot_general(w.astype(BF16), vt, (((1,), (1,)), ((), ())), preferred_element_type=F32)

    @pl.when(p == 0)
    def _():
        cc[...] = jnp.zeros_like(cc)
        acc[...] = jnp.zeros_like(acc)
        row = lax.broadcasted_iota(jnp.int32, (m, t), 0)
        col = lax.broadcasted_iota(jnp.int32, (m, t), 1)
        tile(kn_ref[...], vn_ref[...], col < row // SB_HEADS)

    for r in range(n_pg):
        kt = k_refs[r][...].reshape(SB_WIDTH, t).astype(BF16)
        vt = v_refs[r][...].reshape(SB_WIDTH, t).astype(BF16)
        tile(kt, vt, None)

    @pl.when(p == pl.num_programs(1) - 1)
    def _():
        a = acc[...]
        row = lax.broadcasted_iota(jnp.int32, (m, SB_WIDTH), 0)
        lane = lax.broadcasted_iota(jnp.int32, (m, SB_WIDTH), 1)
        own = (lane // SB_HEAD_DIM) == (row % SB_HEADS)
        o_ref[...] = jnp.sum(jnp.where(own, a, 0.0).reshape(n_q, SB_HEADS, SB_WIDTH), axis=1)


def _sbs_call(layer, page_table, qm, bias_rows, kn, vn, tri, cache_kt, cache_vt, n_q):
    bs, n_pages = page_table.shape
    n_pg = min(PAGES_PER_STEP, n_pages)
    assert n_pages % n_pg == 0
    m = n_q * SB_HEADS
    t = KEY_TILE
    assert cache_kt.shape[-1] == t

    def page_spec(r):
        def imap(b, p, pt):
            return (layer, pt[b, n_pages - 1 - (p * n_pg + r)], 0, 0, 0)
        return pl.BlockSpec((None, None, SB_HEADS, SB_HEAD_DIM, t), imap)

    grid_spec = pltpu.PrefetchScalarGridSpec(
        num_scalar_prefetch=1,
        grid=(bs, n_pages // n_pg),
        in_specs=[pl.BlockSpec((None, m, SB_WIDTH), lambda b, p, pt: (b, 0, 0)),
                  pl.BlockSpec((m, t), lambda b, p, pt: (0, 0)),
                  pl.BlockSpec((None, SB_WIDTH, t), lambda b, p, pt: (b, 0, 0)),
                  pl.BlockSpec((None, SB_WIDTH, t), lambda b, p, pt: (b, 0, 0)),
                  pl.BlockSpec((t, 2 * t), lambda b, p, pt: (0, 0))]
        + [page_spec(r) for r in range(n_pg)] + [page_spec(r) for r in range(n_pg)],
        out_specs=pl.BlockSpec((None, n_q, SB_WIDTH), lambda b, p, pt: (b, 0, 0)),
        scratch_shapes=[pltpu.VMEM((m, SB_WIDTH), F32), pltpu.VMEM((m, t), F32)],
    )
    return pl.pallas_call(
        functools.partial(_sbs_kernel, n_pg=n_pg, n_q=n_q),
        grid_spec=grid_spec,
        out_shape=jax.ShapeDtypeStruct((bs, n_q, SB_WIDTH), F32),
        compiler_params=_cparams(("arbitrary", "arbitrary")),
        name="sb_sample",
    )(page_table, qm, bias_rows, kn, vn, tri, *([cache_kt] * n_pg), *([cache_vt] * n_pg))


def _causal_conv(xp, w_ref, b_ref, n):
    y = b_ref[...] + w_ref[CONV_WIDTH - 1:CONV_WIDTH, :] * xp[ROW_PAD:]
    for j in range(CONV_WIDTH - 1):
        y = y + w_ref[j:j + 1, :] * pltpu.roll(xp, CONV_WIDTH - 1 - j, 0)[ROW_PAD:]
    return y


def _last_rows(xp, t_valid):
    n = xp.shape[0]
    if t_valid % SUBLANES == 0:
        return xp[t_valid:t_valid + ROW_PAD]
    return pltpu.roll(xp, n - t_valid, 0)[:ROW_PAD]


def _neg_expm1(x):
    p = x * (1.0 + x * (1 / 2 + x * (1 / 6 + x * (1 / 24 + x * (1 / 120 + x * (1 / 720 + x * (1 / 5040)))))))
    return jnp.where(x > -0.25, -p, 1.0 - jnp.exp(x))


def _gelu_tanh(x):
    return 0.5 * x * (1.0 + jnp.tanh(math.sqrt(2.0 / math.pi) * (x + 0.044715 * (x * x * x))))


def _rg_kernel(x_ref, g_ref, conv0_ref, h0_ref, cw_ref, cb_ref, wa_ref, ba_ref, wx_ref, bx_ref, spl_ref,
               y_ref, convn_ref, hn_ref, prev_sc, h_sc, hbuf, *, tt, t_valid, from_start):
    ti = pl.program_id(1)

    @pl.when(ti == 0)
    def _():
        prev_sc[...] = conv0_ref[...]
        h_sc[...] = h0_ref[...]

    x = x_ref[...]
    xp = jnp.concatenate([prev_sc[...], x], axis=0)
    xc = _causal_conv(xp, cw_ref, cb_ref, tt)
    convn_ref[...] = _last_rows(xp, t_valid)
    prev_sc[...] = xp[tt:tt + ROW_PAD]

    xb = xc.astype(BF16)
    r = _sigmoid(jnp.dot(xb, wa_ref[...], preferred_element_type=F32) + ba_ref[...])
    ig = _sigmoid(jnp.dot(xb, wx_ref[...], preferred_element_type=F32) + bx_ref[...])
    log_a = -RG_C * r * spl_ref[...]
    a = jnp.exp(log_a)
    mult = jnp.sqrt(_neg_expm1(2.0 * log_a))
    rowi = lax.broadcasted_iota(jnp.int32, (tt, RG_WIDTH), 0)
    if from_start:
        mult = jnp.where(rowi + ti * tt == 0, 1.0, mult)
    u = xc * ig * mult
    s = 1
    while s < tt:
        keep = rowi >= s
        u = jnp.where(keep, u + a * pltpu.roll(u, s, 0), u)
        a = jnp.where(keep, a * pltpu.roll(a, s, 0), a)
        s *= 2
    h = u + a * h_sc[...]
    hbuf[...] = h
    h_last = hbuf[t_valid - 1:t_valid, :]
    h_sc[...] = h_last
    hn_ref[...] = h_last
    y_ref[...] = (_gelu_tanh(g_ref[...]) * h).astype(y_ref.dtype)


def _rg_call(proj, conv0, h0, cw, cb, wa, ba, wx, bx, spl, *, tt, t_valid, from_start):
    b, t, _ = proj.shape
    assert t % tt == 0 and (t_valid == tt or t == tt)
    w = RG_WIDTH
    vec = pl.BlockSpec((1, w), lambda bi, ti: (0, 0))
    mat = pl.BlockSpec((w, w), lambda bi, ti: (0, 0))
    kern = functools.partial(_rg_kernel, tt=tt, t_valid=t_valid, from_start=from_start)
    return pl.pallas_call(
        kern,
        grid=(b, t // tt),
        in_specs=[pl.BlockSpec((None, tt, w), lambda bi, ti: (bi, ti, C_RGX // w)),
                  pl.BlockSpec((None, tt, w), lambda bi, ti: (bi, ti, C_RGG // w)),
                  pl.BlockSpec((None, ROW_PAD, w), lambda bi, ti: (bi, 0, 0)),
                  pl.BlockSpec((None, 1, w), lambda bi, ti: (bi, 0, 0)),
                  pl.BlockSpec((CONV_WIDTH, w), lambda bi, ti: (0, 0)), vec, mat, vec, mat, vec, vec],
        out_specs=[pl.BlockSpec((None, tt, w), lambda bi, ti: (bi, ti, 0)),
                   pl.BlockSpec((None, ROW_PAD, w), lambda bi, ti: (bi, 0, 0)),
                   pl.BlockSpec((None, 1, w), lambda bi, ti: (bi, 0, 0))],
        out_shape=[jax.ShapeDtypeStruct((b, t, w), BF16),
                   jax.ShapeDtypeStruct((b, ROW_PAD, w), F32),
                   jax.ShapeDtypeStruct((b, 1, w), F32)],
        scratch_shapes=[pltpu.VMEM((ROW_PAD, w), F32), pltpu.VMEM((1, w), F32), pltpu.VMEM((tt, w), F32)],
        compiler_params=_cparams(("arbitrary", "arbitrary")),
        name="rg_lru",
    )(proj, proj, conv0, h0, cw, cb, wa, ba, wx, bx, spl)


def _pad_rows(x, n):
    if x.shape[0] == n:
        return x
    return jnp.concatenate([x, jnp.zeros((n - x.shape[0], x.shape[1]), x.dtype)], axis=0)


def _ssd_kernel(z_ref, xbc_ref, dt_ref, conv0_ref, h0_ref, cw_ref, cb_ref, dtb_ref, a_ref, d_ref, nw_ref,
                y_ref, convn_ref, hn_ref, prev_sc, h_sc, *, l_in, t_valid):
    L = SSD_CHUNK
    N = SSD_STATE
    P2 = 2 * SSD_HEAD_DIM
    ci = pl.program_id(1)

    @pl.when(ci == 0)
    def _():
        prev_sc[...] = conv0_ref[...]
        h_sc[...] = h0_ref[...]

    xbc = _pad_rows(xbc_ref[...], L)
    xp = jnp.concatenate([prev_sc[...], xbc], axis=0)
    xc = _causal_conv(xp, cw_ref, cb_ref, L)
    convn_ref[...] = _last_rows(xp, t_valid)
    prev_sc[...] = xp[L:L + ROW_PAD]
    xc = xc * _sigmoid(xc)
    xs = xc[:, :SSD_INNER]
    bm = xc[:, SSD_INNER:SSD_INNER + SSD_GROUPS * N].astype(BF16)
    cm = xc[:, SSD_INNER + SSD_GROUPS * N:].astype(BF16)

    row = lax.broadcasted_iota(jnp.int32, (L, LANES), 0)
    lane = lax.broadcasted_iota(jnp.int32, (L, LANES), 1)
    dtr = _pad_rows(dt_ref[...], L) + dtb_ref[...]
    dt = jnp.maximum(dtr, 0.0) + jnp.log1p(jnp.exp(-jnp.abs(dtr)))
    dt = jnp.where((lane < SSD_HEADS) & (row < t_valid), dt, 0.0)
    cs = dt * a_ref[...]
    s = 1
    while s < L:
        cs = jnp.where(row >= s, cs + pltpu.roll(cs, s, 0), cs)
        s *= 2
    ecs = jnp.exp(cs)
    cs_t = cs.T
    dt_t = dt.T
    xs_t = xs.T
    tril = row >= lane
    first_head = lane < SSD_HEAD_DIM
    first_rows = row < SSD_HEAD_DIM

    y_pairs = []
    for hp in range(SSD_HEADS // 2):
        g = (2 * hp) // (SSD_HEADS // SSD_GROUPS)
        bm_g = bm[:, g * N:(g + 1) * N]
        cm_g = cm[:, g * N:(g + 1) * N]
        cb = lax.dot_general(cm_g, bm_g, (((1,), (1,)), ((), ())), preferred_element_type=F32)
        x2 = xs[:, hp * P2:(hp + 1) * P2]
        x2b = x2.astype(BF16)
        yd, wrow, cdec, eoff = [], [], [], []
        for hh in range(2):
            h = 2 * hp + hh
            seg = cs[:, h:h + 1] - cs_t[h:h + 1, :]
            decay = jnp.exp(jnp.where(tril, seg, -jnp.inf))
            wts = cb * decay * dt_t[h:h + 1, :]
            yd.append(jnp.dot(wts.astype(BF16), x2b, preferred_element_type=F32))
            last = cs_t[h:h + 1, L - 1:L]
            wrow.append(jnp.exp(last - cs_t[h:h + 1, :]) * dt_t[h:h + 1, :])
            cdec.append(jnp.exp(last))
            eoff.append(ecs[:, h:h + 1])
        h_pair = h_sc[hp * P2:(hp + 1) * P2, :]
        y_off = lax.dot_general(cm_g, h_pair.astype(BF16), (((1,), (1,)), ((), ())),
                                preferred_element_type=F32)
        y_off = y_off * jnp.where(first_head, eoff[0], eoff[1])
        y_pairs.append(jnp.where(first_head, yd[0], yd[1]) + y_off + d_ref[:, hp * P2:(hp + 1) * P2] * x2)
        xw = xs_t[hp * P2:(hp + 1) * P2, :] * jnp.where(first_rows, wrow[0], wrow[1])
        st = jnp.dot(xw.astype(BF16), bm_g, preferred_element_type=F32)
        h_sc[hp * P2:(hp + 1) * P2, :] = jnp.where(first_rows, cdec[0], cdec[1]) * h_pair + st

    hn_ref[...] = h_sc[...]
    y = jnp.concatenate(y_pairs, axis=1)[:l_in]
    zz = z_ref[...]
    u = y * (zz * _sigmoid(zz))
    gw = SSD_INNER // SSD_GROUPS
    u = jnp.concatenate([_rms(u[:, g * gw:(g + 1) * gw]) for g in range(SSD_GROUPS)], axis=1)
    y_ref[...] = (u * nw_ref[...]).astype(y_ref.dtype)


def _ssd_call(proj, conv0, h0, cw, cb, dtb, a_row, d_lane, nw, *, l_in, t_valid):
    b, t, _ = proj.shape
    assert t % l_in == 0 and (l_in == SSD_CHUNK or t == l_in)
    c = SSD_CONV_DIM
    hp_rows = SSD_HEADS * SSD_HEAD_DIM
    kern = functools.partial(_ssd_kernel, l_in=l_in, t_valid=t_valid)
    return pl.pallas_call(
        kern,
        grid=(b, t // l_in),
        in_specs=[pl.BlockSpec((None, l_in, SSD_INNER), lambda bi, ci: (bi, ci, C_Z // SSD_INNER)),
                  pl.BlockSpec((None, l_in, c), lambda bi, ci: (bi, ci, C_XBC // c)),
                  pl.BlockSpec((None, l_in, LANES), lambda bi, ci: (bi, ci, C_DT // LANES)),
                  pl.BlockSpec((None, ROW_PAD, c), lambda bi, ci: (bi, 0, 0)),
                  pl.BlockSpec((None, hp_rows, SSD_STATE), lambda bi, ci: (bi, 0, 0)),
                  pl.BlockSpec((CONV_WIDTH, c), lambda bi, ci: (0, 0)),
                  pl.BlockSpec((1, c), lambda bi, ci: (0, 0)),
                  pl.BlockSpec((1, LANES), lambda bi, ci: (0, 0)),
                  pl.BlockSpec((1, LANES), lambda bi, ci: (0, 0)),
                  pl.BlockSpec((1, SSD_INNER), lambda bi, ci: (0, 0)),
                  pl.BlockSpec((1, SSD_INNER), lambda bi, ci: (0, 0))],
        out_specs=[pl.BlockSpec((None, l_in, SSD_INNER), lambda bi, ci: (bi, ci, 0)),
                   pl.BlockSpec((None, ROW_PAD, c), lambda bi, ci: (bi, 0, 0)),
                   pl.BlockSpec((None, hp_rows, SSD_STATE), lambda bi, ci: (bi, 0, 0))],
        out_shape=[jax.ShapeDtypeStruct((b, t, SSD_INNER), BF16),
                   jax.ShapeDtypeStruct((b, ROW_PAD, c), F32),
                   jax.ShapeDtypeStruct((b, hp_rows, SSD_STATE), F32)],
        scratch_shapes=[pltpu.VMEM((ROW_PAD, c), F32), pltpu.VMEM((hp_rows, SSD_STATE), F32)],
        compiler_params=_cparams(("arbitrary", "arbitrary")),
        name="ssd",
    )(proj, proj, proj, conv0, h0, cw, cb, dtb, a_row, d_lane, nw)


def _merge_kernel(ya_ref, yb_ref, yc_ref, g0_ref, g1_ref, g2_ref, x_ref, gt_ref, sc_ref, sh_ref,
                  wa_ref, wb_ref, wc_ref, wo_ref, nf_ref, wr_ref, br_ref, x1_ref, h2_ref, lg_ref):
    def branch(y_ref, w_ref, g_ref):
        return _sigmoid(g_ref[...]) * jnp.dot(y_ref[...], w_ref[...], preferred_element_type=F32)

    m = branch(ya_ref, wa_ref, g0_ref) + branch(yb_ref, wb_ref, g1_ref) + branch(yc_ref, wc_ref, g2_ref)
    o = jnp.dot(m.astype(BF16), wo_ref[...], preferred_element_type=F32)
    x1 = x_ref[...] + gt_ref[...] * o
    x1_ref[...] = x1
    h2 = _rms(x1) * nf_ref[...] * (1.0 + sc_ref[...]) + sh_ref[...]
    h2_ref[...] = h2.astype(h2_ref.dtype)
    lg_ref[...] = jnp.dot(h2, wr_ref[...], precision=HIGHEST, preferred_element_type=F32) + br_ref[...]


def _merge_call(ya, yb, yc, proj, x, gt, sc, sh, wa, wb, wc, wo, nf, wr, br, *, per_row, rows_per_group, tm):
    r = x.shape[0]
    assert r % tm == 0 and (per_row or rows_per_group % tm == 0)
    d = D_MODEL
    ms = _mod_spec(per_row, tm, rows_per_group)
    yspec = pl.BlockSpec((tm, SB_WIDTH), lambda i: (i, 0))
    gspec = [pl.BlockSpec((tm, d), lambda i, k=k: (i, C_BRG // d + k)) for k in range(N_BRANCH)]
    wspec = pl.BlockSpec((SB_WIDTH, d), lambda i: (0, 0))
    return pl.pallas_call(
        _merge_kernel,
        grid=(r // tm,),
        in_specs=[yspec, yspec, yspec, *gspec, pl.BlockSpec((tm, d), lambda i: (i, 0)), ms, ms, ms,
                  wspec, wspec, wspec, pl.BlockSpec((d, d), lambda i: (0, 0)),
                  pl.BlockSpec((1, d), lambda i: (0, 0)),
                  pl.BlockSpec((d, LANES), lambda i: (0, 0)), pl.BlockSpec((1, LANES), lambda i: (0, 0))],
        out_specs=[pl.BlockSpec((tm, d), lambda i: (i, 0)), pl.BlockSpec((tm, d), lambda i: (i, 0)),
                   pl.BlockSpec((tm, LANES), lambda i: (i, 0))],
        out_shape=[jax.ShapeDtypeStruct((r, d), F32), jax.ShapeDtypeStruct((r, d), BF16),
                   jax.ShapeDtypeStruct((r, LANES), F32)],
        compiler_params=_cparams(("arbitrary",)),
        name="merge",
    )(ya, yb, yc, proj, proj, proj, x, gt, sc, sh, wa, wb, wc, wo, nf, wr, br)


def _moe_kernel(be_ref, xs_ref, wg_ref, wl_ref, bg_ref, bl_ref, wd_ref, bd_ref, gate_ref, o_ref):
    del be_ref
    x = xs_ref[...]
    glu = jnp.minimum(jnp.dot(x, wg_ref[...], preferred_element_type=F32) + bg_ref[...], SWIGLU_LIMIT)
    lin = jnp.clip(jnp.dot(x, wl_ref[...], preferred_element_type=F32) + bl_ref[...], -SWIGLU_LIMIT, SWIGLU_LIMIT)
    act = glu * _sigmoid(SWIGLU_ALPHA * glu) * (lin + 1.0)
    y = jnp.dot(act.astype(BF16), wd_ref[...], preferred_element_type=F32) + bd_ref[...]
    o_ref[...] = y * gate_ref[...]


def _moe_call(layer, block_exp, xs, wg, wl, bg, bl, wd, bd, row_gate):
    r = xs.shape[0]
    d = D_MODEL
    wspec = pl.BlockSpec((None, None, d, d), lambda i, be: (layer, be[i], 0, 0))
    bspec = pl.BlockSpec((None, None, 1, d), lambda i, be: (layer, be[i], 0, 0))
    grid_spec = pltpu.PrefetchScalarGridSpec(
        num_scalar_prefetch=1,
        grid=(r // MOE_BM,),
        in_specs=[pl.BlockSpec((MOE_BM, d), lambda i, be: (i, 0)), wspec, wspec, bspec, bspec, wspec, bspec,
                  pl.BlockSpec((MOE_BM, 1), lambda i, be: (i, 0))],
        out_specs=pl.BlockSpec((MOE_BM, d), lambda i, be: (i, 0)),
    )
    return pl.pallas_call(
        _moe_kernel,
        grid_spec=grid_spec,
        out_shape=jax.ShapeDtypeStruct((r, d), F32),
        compiler_params=_cparams(("arbitrary",)),
        name="moe",
    )(block_exp, xs, wg, wl, bg, bl, wd, bd, row_gate)


def _route(logits, bm):
    n_tok = logits.shape[0]
    n_slot = n_tok * TOP_K
    n_blocks = -(-(n_slot + N_EXPERTS * (bm - 1)) // bm)
    n_rows = n_blocks * bm
    top_val, top_idx = lax.top_k(logits, TOP_K)
    gate = jax.nn.softmax(top_val, axis=-1).reshape(n_slot)
    e_flat = top_idx.reshape(n_slot).astype(jnp.int32)
    order = jnp.argsort(e_flat, stable=True).astype(jnp.int32)
    rank = jnp.argsort(order).astype(jnp.int32)
    experts = jnp.arange(N_EXPERTS, dtype=jnp.int32)
    counts = jnp.sum((e_flat[:, None] == experts[None, :]).astype(jnp.int32), axis=0)
    padded = (counts + bm - 1) // bm * bm
    start = jnp.cumsum(counts) - counts
    pend = jnp.cumsum(padded)
    shift = (pend - padded) - start
    block_row = jnp.arange(n_blocks, dtype=jnp.int32) * bm
    block_exp = jnp.minimum(jnp.sum((pend[None, :] <= block_row[:, None]).astype(jnp.int32), axis=1),
                            N_EXPERTS - 1)
    rows = jnp.arange(n_rows, dtype=jnp.int32)
    row_exp = jnp.repeat(block_exp, bm)
    valid = (rows - (pend - padded)[row_exp]) < counts[row_exp]
    valid = valid & (rows < pend[N_EXPERTS - 1])
    row_slot = order[jnp.clip(rows - shift[row_exp], 0, n_slot - 1)]
    row_tok = jnp.where(valid, row_slot // TOP_K, 0)
    row_gate = jnp.where(valid, gate[row_slot], 0.0)
    slot_row = rank + shift[e_flat]
    return block_exp, row_tok, row_gate, slot_row.reshape(n_tok, TOP_K)


def _resid_kernel(x_ref, f_ref, gt_ref, nw_ref, o_ref, *, final):
    x = x_ref[...] + gt_ref[...] * f_ref[...]
    if final:
        x = _rms(x) * nw_ref[...]
    o_ref[...] = x


def _resid_call(x, f, gt, nw, *, final, per_row, rows_per_group, tm):
    r = x.shape[0]
    d = D_MODEL
    row = pl.BlockSpec((tm, d), lambda i: (i, 0))
    return pl.pallas_call(
        functools.partial(_resid_kernel, final=final),
        grid=(r // tm,),
        in_specs=[row, row, _mod_spec(per_row, tm, rows_per_group), pl.BlockSpec((1, d), lambda i: (0, 0))],
        out_specs=row,
        out_shape=jax.ShapeDtypeStruct((r, d), F32),
        compiler_params=_cparams(("arbitrary",)),
        name="resid",
    )(x, f, gt, nw)


def _block_diag(w):
    depth, nb, k, _ = w.shape
    eye = jnp.eye(nb, dtype=w.dtype)
    return jnp.einsum('lnij,nm->lnimj', w, eye).reshape(depth, nb * k, nb * k)


def _pick_tile(n, pref):
    t = min(pref, n)
    while n % t:
        t //= 2
    return t


def kernel(x_prompt, x_sample, cache_k, cache_v, state_rg_conv, state_rg_h, state_ssd_conv, state_ssd, page_table, c_prompt, c_sample, w_mod, b_mod, norm_mix, norm_ffn, norm_final, w_in, sb_bias, rg_conv_w, rg_conv_b, rg_w_a, rg_b_a, rg_w_x, rg_b_x, rg_lam, ssd_conv_w, ssd_conv_b, ssd_dt_bias, ssd_a_log, ssd_d, ssd_norm, w_branch, w_out, w_router, b_router, w_up, b_up, w_down, b_down):
    depth = w_in.shape[0]
    b, s, d = x_prompt.shape
    bs, ts, _ = x_sample.shape
    tp = ROW_PAD
    n_p, n_s = b * s, bs * tp
    past_len = page_table.shape[1] * cache_k.shape[2]

    w_in_p = jnp.concatenate(
        [w_in[:, :, :4096], w_in[:, :, 4104:], w_in[:, :, 4096:4104],
         jnp.zeros((depth, d, NP_COLS - w_in.shape[2]), w_in.dtype)], axis=2).astype(BF16)
    wb = w_branch.astype(BF16)
    wb_a, wb_b, wb_c = wb[:, :SB_WIDTH], wb[:, SB_WIDTH:SB_WIDTH + RG_WIDTH], wb[:, SB_WIDTH + RG_WIDTH:]
    wo = w_out.astype(BF16)
    wr = jnp.pad(w_router, ((0, 0), (0, 0), (0, LANES - N_EXPERTS)))
    br = jnp.pad(b_router, ((0, 0), (0, LANES - N_EXPERTS))).reshape(depth, 1, LANES)
    wg = w_up[..., 0::2].astype(BF16)
    wl = w_up[..., 1::2].astype(BF16)
    bg = b_up[..., 0::2].reshape(depth, N_EXPERTS, 1, D_FF)
    bl = b_up[..., 1::2].reshape(depth, N_EXPERTS, 1, D_FF)
    wd = w_down.astype(BF16)
    bd = b_down.reshape(depth, N_EXPERTS, 1, d)
    rg_wa = _block_diag(rg_w_a).astype(BF16)
    rg_wx = _block_diag(rg_w_x).astype(BF16)
    rg_spl = jax.nn.softplus(-rg_lam)
    ssd_a = jnp.pad(-jnp.exp(ssd_a_log), ((0, 0), (0, LANES - SSD_HEADS)))
    ssd_dtb = jnp.pad(ssd_dt_bias, ((0, 0), (0, LANES - SSD_HEADS)))
    ssd_d_lane = jnp.repeat(ssd_d, SSD_HEAD_DIM, axis=1)
    tri = _tri_matrix(KEY_TILE)
    cache_kt = jnp.transpose(cache_k, (0, 1, 3, 4, 2))
    cache_vt = jnp.transpose(cache_v, (0, 1, 3, 4, 2))

    mod = _mod_call(jnp.concatenate([c_prompt, c_sample], axis=0), w_mod, b_mod)

    def row2(v):
        return v.reshape(1, -1)

    def pad_prev(st):
        return jnp.pad(st, ((0, 0), (tp - (CONV_WIDTH - 1), 0), (0, 0)))

    xp = x_prompt.reshape(n_p, d)
    xq = jnp.pad(x_sample, ((0, 0), (0, tp - ts), (0, 0))).reshape(n_s, d)
    tm_in = _pick_tile(s, 1024)
    tm_mg = _pick_tile(s, 256)
    tt_rg = _pick_tile(s, 256)
    tm_rs = _pick_tile(s, 512)
    outs_p, outs_s = [], []
    for l in range(depth):
        mp = mod[l, :b].reshape(b, 1, 6, d)
        mq = jnp.repeat(mod[l, b:], tp, axis=0).reshape(n_s, 6, d)
        sh1p, sc1p, gt1p, sh2p, sc2p, gt2p = (mp[:, :, k] for k in range(6))
        sh1q, sc1q, gt1q, sh2q, sc2q, gt2q = (mq[:, k] for k in range(6))
        bias_l = sb_bias[l]

        proj_p = _inproj_call(xp, sc1p, sh1p, row2(norm_mix[l]), w_in_p[l],
                              per_row=False, rows_per_group=s, tm=tm_in).reshape(b, s, NP_COLS)
        ya_p = _sbp_call(proj_p, bias_l, tri)
        yb_p, rgc_p, rgh_p = _rg_call(
            proj_p, jnp.zeros((b, tp, RG_WIDTH), F32), jnp.zeros((b, 1, RG_WIDTH), F32),
            rg_conv_w[l], row2(rg_conv_b[l]), rg_wa[l], row2(rg_b_a[l]), rg_wx[l], row2(rg_b_x[l]),
            row2(rg_spl[l]), tt=tt_rg, t_valid=tt_rg, from_start=True)
        yc_p, ssc_p, ss_p = _ssd_call(
            proj_p, jnp.zeros((b, tp, SSD_CONV_DIM), F32),
            jnp.zeros((b, SSD_HEADS * SSD_HEAD_DIM, SSD_STATE), F32),
            ssd_conv_w[l], row2(ssd_conv_b[l]), row2(ssd_dtb[l]), row2(ssd_a[l]), row2(ssd_d_lane[l]),
            row2(ssd_norm[l]), l_in=SSD_CHUNK, t_valid=SSD_CHUNK)
        x1p, h2p, lgp = _merge_call(
            ya_p.reshape(n_p, -1), yb_p.reshape(n_p, -1), yc_p.reshape(n_p, -1), proj_p.reshape(n_p, NP_COLS),
            xp, gt1p, sc2p, sh2p, wb_a[l], wb_b[l], wb_c[l], wo[l], row2(norm_ffn[l]), wr[l], br[l],
            per_row=False, rows_per_group=s, tm=tm_mg)
        outs_p.append((proj_p[:, :, C_K:C_K + SB_WIDTH], proj_p[:, :, C_V:C_V + SB_WIDTH],
                       rgc_p[:, tp - 3:], rgh_p[:, 0], ssc_p[:, tp - 3:], ss_p))

        proj_q = _inproj_call(xq, sc1q, sh1q, row2(norm_mix[l]), w_in_p[l],
                              per_row=True, rows_per_group=tp, tm=n_s).reshape(bs, tp, NP_COLS)
        q_new = proj_q[:, :ts, C_Q:C_Q + SB_WIDTH] * (SB_HEAD_DIM ** -0.5)
        k_new = proj_q[:, :ts, C_K:C_K + SB_WIDTH]
        v_new = proj_q[:, :ts, C_V:C_V + SB_WIDTH]
        head_of_col = jnp.arange(SB_WIDTH) // SB_HEAD_DIM
        own = head_of_col[None, :] == jnp.arange(SB_HEADS)[:, None]
        qm = jnp.where(own[None, None], q_new[:, :, None, :], 0.0).reshape(bs, ts * SB_HEADS, SB_WIDTH)
        bias_rows = jnp.broadcast_to(jnp.tile(bias_l, ts)[:, None], (ts * SB_HEADS, KEY_TILE))
        kn = jnp.pad(jnp.swapaxes(k_new, 1, 2), ((0, 0), (0, 0), (0, KEY_TILE - ts))).astype(BF16)
        vn = jnp.pad(jnp.swapaxes(v_new, 1, 2), ((0, 0), (0, 0), (0, KEY_TILE - ts))).astype(BF16)
        ya_q = _sbs_call(l, page_table, qm.astype(BF16), bias_rows, kn, vn, tri, cache_kt, cache_vt, ts)
        ya_q = jnp.pad(ya_q, ((0, 0), (0, tp - ts), (0, 0))).astype(BF16)
        yb_q, rgc_q, rgh_q = _rg_call(
            proj_q, pad_prev(state_rg_conv[l]), state_rg_h[l][:, None, :],
            rg_conv_w[l], row2(rg_conv_b[l]), rg_wa[l], row2(rg_b_a[l]), rg_wx[l], row2(rg_b_x[l]),
            row2(rg_spl[l]), tt=tp, t_valid=ts, from_start=(past_len == 0))
        yc_q, ssc_q, ss_q = _ssd_call(
            proj_q, pad_prev(state_ssd_conv[l]),
            state_ssd[l].reshape(bs, SSD_HEADS * SSD_HEAD_DIM, SSD_STATE),
            ssd_conv_w[l], row2(ssd_conv_b[l]), row2(ssd_dtb[l]), row2(ssd_a[l]), row2(ssd_d_lane[l]),
            row2(ssd_norm[l]), l_in=tp, t_valid=ts)
        x1q, h2q, lgq = _merge_call(
            ya_q.reshape(n_s, -1), yb_q.reshape(n_s, -1), yc_q.reshape(n_s, -1), proj_q.reshape(n_s, NP_COLS),
            xq, gt1q, sc2q, sh2q, wb_a[l], wb_b[l], wb_c[l], wo[l], row2(norm_ffn[l]), wr[l], br[l],
            per_row=True, rows_per_group=tp, tm=n_s)
        outs_s.append((k_new, v_new, rgc_q[:, tp - 3:], rgh_q[:, 0], ssc_q[:, tp - 3:], ss_q))

        h2 = jnp.concatenate([h2p, h2q], axis=0)
        logits = jnp.concatenate([lgp, lgq], axis=0)[:, :N_EXPERTS]
        block_exp, row_tok, row_gate, slot_row = _route(logits, MOE_BM)
        ys = _moe_call(l, block_exp, h2[row_tok], wg, wl, bg, bl, wd, bd, row_gate[:, None])
        f = jnp.sum(ys[slot_row], axis=1)
        final = l == depth - 1
        xp = _resid_call(x1p, f[:n_p], gt2p, row2(norm_final), final=final,
                         per_row=False, rows_per_group=s, tm=tm_rs)
        xq = _resid_call(x1q, f[n_p:], gt2q, row2(norm_final), final=final,
                         per_row=True, rows_per_group=tp, tm=n_s)

    def stack(outs, i, shape):
        return jnp.stack([o[i] for o in outs]).reshape(shape)

    y_prompt = xp.reshape(b, s, d)
    y_sample = xq.reshape(bs, tp, d)[:, :ts]
    hs = (SSD_HEADS, SSD_HEAD_DIM, SSD_STATE)
    return (y_prompt, y_sample,
            stack(outs_p, 0, (depth, b, s, SB_HEADS, SB_HEAD_DIM)),
            stack(outs_p, 1, (depth, b, s, SB_HEADS, SB_HEAD_DIM)),
            stack(outs_p, 2, (depth, b, CONV_WIDTH - 1, RG_WIDTH)), stack(outs_p, 3, (depth, b, RG_WIDTH)),
            stack(outs_p, 4, (depth, b, CONV_WIDTH - 1, SSD_CONV_DIM)), stack(outs_p, 5, (depth, b) + hs),
            stack(outs_s, 0, (depth, bs, ts, SB_HEADS, SB_HEAD_DIM)),
            stack(outs_s, 1, (depth, bs, ts, SB_HEADS, SB_HEAD_DIM)),
            stack(outs_s, 2, (depth, bs, CONV_WIDTH - 1, RG_WIDTH)), stack(outs_s, 3, (depth, bs, RG_WIDTH)),
            stack(outs_s, 4, (depth, bs, CONV_WIDTH - 1, SSD_CONV_DIM)), stack(outs_s, 5, (depth, bs) + hs))
```

```python
import functools
import math

import jax
import jax.numpy as jnp
from jax import lax
from jax.experimental import pallas as pl
from jax.experimental.pallas import tpu as pltpu

F32 = jnp.float32
BF16 = jnp.bfloat16
HIGHEST = lax.Precision.HIGHEST

D_MODEL = 1024
SB_HEADS = 8
SB_HEAD_DIM = 64
SB_WIDTH = SB_HEADS * SB_HEAD_DIM
RG_WIDTH = 512
RG_BLOCKS = 8
RG_C = 8.0
CONV_WIDTH = 4
SSD_INNER = 512
SSD_HEAD_DIM = 64
SSD_HEADS = SSD_INNER // SSD_HEAD_DIM
SSD_GROUPS = 2
SSD_STATE = 128
SSD_CHUNK = 128
SSD_CONV_DIM = SSD_INNER + 2 * SSD_GROUPS * SSD_STATE
N_BRANCH = 3
N_EXPERTS = 32
TOP_K = 4
D_FF = D_MODEL
SWIGLU_LIMIT = 7.0
SWIGLU_ALPHA = 1.702
EPS = 1e-6

SUBLANES = 8
LANES = 128
VMEM_LIMIT = 48 * 1024 * 1024

C_Q, C_K, C_V, C_RGX, C_RGG, C_Z, C_XBC, C_BRG, C_DT = 0, 512, 1024, 1536, 2048, 2560, 3072, 4096, 7168
NP_COLS = 7680
IN_TN = 1536
ROW_PAD = 8
KEY_TILE = 128
MOE_BM = 256
PAGES_PER_STEP = 8
SBP_SUBTILES = 4
MOE_VMEM_LIMIT = 56 * 1024 * 1024


def _cparams(sem, vmem=VMEM_LIMIT):
    return pltpu.CompilerParams(dimension_semantics=sem, vmem_limit_bytes=vmem)


def _sigmoid(x):
    return 1.0 / (1.0 + jnp.exp(-x))


def _rms(x):
    return x * lax.rsqrt(jnp.mean(x * x, axis=-1, keepdims=True) + EPS)


def _mod_kernel(c_ref, w_ref, b_ref, o_ref):
    c = c_ref[...]
    s = c * _sigmoid(c)
    o_ref[...] = jnp.dot(s, w_ref[...], precision=HIGHEST, preferred_element_type=F32) + b_ref[...]


def _mod_call(c_all, w_mod, b_mod):
    depth, d, n = w_mod.shape
    nc = c_all.shape[0]
    tn = 1536
    return pl.pallas_call(
        _mod_kernel,
        grid=(depth, n // tn),
        in_specs=[pl.BlockSpec((nc, d), lambda l, j: (0, 0)),
                  pl.BlockSpec((None, d, tn), lambda l, j: (l, 0, j)),
                  pl.BlockSpec((None, 1, tn), lambda l, j: (l, 0, j))],
        out_specs=pl.BlockSpec((None, nc, tn), lambda l, j: (l, 0, j)),
        out_shape=jax.ShapeDtypeStruct((depth, nc, n), F32),
        compiler_params=_cparams(("arbitrary", "arbitrary")),
        name="mod",
    )(c_all, w_mod, b_mod.reshape(depth, 1, n))


def _inproj_kernel(x_ref, sc_ref, sh_ref, g_ref, w_ref, o_ref, h_ref):
    @pl.when(pl.program_id(1) == 0)
    def _():
        h = _rms(x_ref[...]) * g_ref[...]
        h_ref[...] = (h * (1.0 + sc_ref[...]) + sh_ref[...]).astype(BF16)

    o_ref[...] = jnp.dot(h_ref[...], w_ref[...], preferred_element_type=F32)


def _mod_spec(per_row, tm, rows_per_group):
    if per_row:
        return pl.BlockSpec((tm, D_MODEL), lambda i, *_: (i, 0))
    return pl.BlockSpec((None, 1, D_MODEL), lambda i, *_: ((i * tm) // rows_per_group, 0, 0))


def _inproj_call(x, sc, sh, g, w, *, per_row, rows_per_group, tm):
    r = x.shape[0]
    assert r % tm == 0 and (per_row or rows_per_group % tm == 0)
    ms = _mod_spec(per_row, tm, rows_per_group)
    return pl.pallas_call(
        _inproj_kernel,
        grid=(r // tm, NP_COLS // IN_TN),
        in_specs=[pl.BlockSpec((tm, D_MODEL), lambda i, j: (i, 0)), ms, ms,
                  pl.BlockSpec((1, D_MODEL), lambda i, j: (0, 0)),
                  pl.BlockSpec((D_MODEL, IN_TN), lambda i, j: (0, j))],
        out_specs=pl.BlockSpec((tm, IN_TN), lambda i, j: (i, j)),
        out_shape=jax.ShapeDtypeStruct((r, NP_COLS), F32),
        scratch_shapes=[pltpu.VMEM((tm, D_MODEL), BF16)],
        compiler_params=_cparams(("arbitrary", "arbitrary")),
        name="inproj",
    )(x, sc, sh, g, w)


def _sb_logs(z, mask=None):
    sp = jnp.log(1.0 + jnp.exp(-jnp.abs(z)))
    m = jnp.minimum(z, 0.0)
    lk = (m - z) - sp
    if mask is not None:
        lk = jnp.where(mask, lk, 0.0)
    return m - sp, lk


def _sb_mass(lk, tri):
    return jnp.dot(lk.astype(BF16), tri, preferred_element_type=F32)


def _sb_tile(z, tri, c_old, mask):
    t = z.shape[1]
    ls, lk = _sb_logs(z, mask)
    r = _sb_mass(lk, tri)
    w = jnp.exp(ls + r[:, :t] + c_old)
    if mask is not None:
        w = jnp.where(mask, w, 0.0)
    return w, c_old + r[:, t:]


def _tri_matrix(t):
    j = lax.broadcasted_iota(jnp.int32, (t, 2 * t), 0)
    s = lax.broadcasted_iota(jnp.int32, (t, 2 * t), 1)
    return jnp.where((j > s) | (s >= t), 1.0, 0.0).astype(BF16)


def _sbp_kernel(bias_ref, q_ref, k_ref, v_ref, tri_ref, o_ref, kb, vb, qm, acc, cc, *, scale, nq):
    t = KEY_TILE
    hp = pl.program_id(1)
    i = pl.program_id(2)

    @pl.when(i == 0)
    def _():
        kb[...] = k_ref[...].astype(BF16)
        vb[...] = v_ref[...].astype(BF16)

    lane = lax.broadcasted_iota(jnp.int32, (nq * t, LANES), 1)
    first_head = lane < SB_HEAD_DIM
    q = q_ref[...] * scale
    qm[0] = jnp.where(first_head, q, 0.0).astype(BF16)
    qm[1] = jnp.where(first_head, 0.0, q).astype(BF16)
    cc[...] = jnp.zeros_like(cc)
    acc[...] = jnp.zeros_like(acc)
    tri = tri_ref[...]
    row = lax.broadcasted_iota(jnp.int32, (t, t), 0)
    col = lax.broadcasted_iota(jnp.int32, (t, t), 1)
    causal = col < row
    bias = (bias_ref[2 * hp], bias_ref[2 * hp + 1])

    def tile(j, subs):
        start = pl.multiple_of(j * t, t)
        kt = kb[pl.ds(start, t), :]
        vt = vb[pl.ds(start, t), :]
        chains = [(hh, slice(r * t, (r + 1) * t), causal if masked else None)
                  for r, masked in subs for hh in range(2)]
        zs = [lax.dot_general(qm[hh, rows], kt, (((1,), (1,)), ((), ())), preferred_element_type=F32) + bias[hh]
              for hh, rows, _ in chains]
        logs = [_sb_logs(z, mask) for z, (_, _, mask) in zip(zs, chains)]
        mass = [_sb_mass(lk, tri) for _, lk in logs]
        pvs = []
        for (hh, rows, mask), (ls, _), r in zip(chains, logs, mass):
            c_old = cc[hh, rows]
            w = jnp.exp(ls + r[:, :t] + c_old)
            if mask is not None:
                w = jnp.where(mask, w, 0.0)
            cc[hh, rows] = c_old + r[:, t:]
            pvs.append(jnp.dot(w.astype(BF16), vt, preferred_element_type=F32))
        for (hh, rows, _), pv in zip(chains, pvs):
            acc[hh, rows] += pv

    for c in range(nq - 1, -1, -1):
        tile(i * nq + c, [(r, r == c) for r in range(c, nq)])

    def body(jj, carry):
        tile(i * nq - 1 - jj, [(r, False) for r in range(nq)])
        return carry

    lax.fori_loop(0, i * nq, body, 0)
    o_ref[...] = jnp.where(first_head, acc[0], acc[1]).astype(o_ref.dtype)


def _sbp_call(proj, bias, tri):
    b, s, _ = proj.shape
    t = KEY_TILE
    nq = _pick_tile(s // t, SBP_SUBTILES)
    tq = nq * t
    assert s % tq == 0
    kern = functools.partial(_sbp_kernel, scale=SB_HEAD_DIM ** -0.5, nq=nq)
    return pl.pallas_call(
        kern,
        grid=(b, SB_WIDTH // LANES, s // tq),
        in_specs=[pl.BlockSpec(memory_space=pltpu.SMEM),
                  pl.BlockSpec((None, tq, LANES), lambda bi, hp, i: (bi, i, C_Q // LANES + hp)),
                  pl.BlockSpec((None, s, LANES), lambda bi, hp, i: (bi, 0, C_K // LANES + hp)),
                  pl.BlockSpec((None, s, LANES), lambda bi, hp, i: (bi, 0, C_V // LANES + hp)),
                  pl.BlockSpec((t, 2 * t), lambda bi, hp, i: (0, 0))],
        out_specs=pl.BlockSpec((None, tq, LANES), lambda bi, hp, i: (bi, i, hp)),
        out_shape=jax.ShapeDtypeStruct((b, s, SB_WIDTH), BF16),
        scratch_shapes=[pltpu.VMEM((s, LANES), BF16), pltpu.VMEM((s, LANES), BF16),
                        pltpu.VMEM((2, tq, LANES), BF16),
                        pltpu.VMEM((2, tq, LANES), F32), pltpu.VMEM((2, tq, LANES), F32)],
        compiler_params=_cparams(("arbitrary", "arbitrary", "arbitrary")),
        name="sb_prompt",
    )(bias, proj, proj, proj, tri)


def _sbs_kernel(pt_ref, qm_ref, bias_ref, kn_ref, vn_ref, tri_ref, *rest, n_pg, n_q):
    del pt_ref
    k_refs = rest[:n_pg]
    v_refs = rest[n_pg:2 * n_pg]
    o_ref, acc, cc = rest[2 * n_pg:]
    t = KEY_TILE
    m = n_q * SB_HEADS
    p = pl.program_id(1)
    qm = qm_ref[...]
    bias = bias_ref[...]
    tri = tri_ref[...]

    def pv(w, vt):
        return lax.dot_general(w.astype(BF16), vt, (((1,), (1,)), ((), ())), preferred_element_type=F32)

    @pl.when(p == 0)
    def _():
        row = lax.broadcasted_iota(jnp.int32, (m, t), 0)
        col = lax.broadcasted_iota(jnp.int32, (m, t), 1)
        z = jnp.dot(qm, kn_ref[...], preferred_element_type=F32) + bias
        w, c_new = _sb_tile(z, tri, jnp.zeros((m, t), F32), col < row // SB_HEADS)
        cc[...] = c_new
        acc[...] = pv(w, vn_ref[...])

    logs = [_sb_logs(jnp.dot(qm, k_refs[r][...].reshape(SB_WIDTH, t).astype(BF16),
                             preferred_element_type=F32) + bias) for r in range(n_pg)]
    mass = [_sb_mass(lk, tri) for _, lk in logs]
    c = cc[...]
    out = acc[...]
    for r in range(n_pg):
        w = jnp.exp(logs[r][0] + mass[r][:, :t] + c)
        c = c + mass[r][:, t:]
        out = out + pv(w, v_refs[r][...].reshape(SB_WIDTH, t).astype(BF16))
    cc[...] = c
    acc[...] = out

    @pl.when(p == pl.num_programs(1) - 1)
    def _():
        a = acc[...]
        row = lax.broadcasted_iota(jnp.int32, (m, SB_WIDTH), 0)
        lane = lax.broadcasted_iota(jnp.int32, (m, SB_WIDTH), 1)
        own = (lane // SB_HEAD_DIM) == (row % SB_HEADS)
        o_ref[...] = jnp.sum(jnp.where(own, a, 0.0).reshape(n_q, SB_HEADS, SB_WIDTH), axis=1)


def _sbs_call(layer, page_table, qm, bias_rows, kn, vn, tri, cache_kt, cache_vt, n_q):
    bs, n_pages = page_table.shape
    n_pg = min(PAGES_PER_STEP, n_pages)
    assert n_pages % n_pg == 0
    m = n_q * SB_HEADS
    t = KEY_TILE
    assert cache_kt.shape[-1] == t

    def page_spec(r):
        def imap(b, p, pt):
            return (layer, pt[b, n_pages - 1 - (p * n_pg + r)], 0, 0, 0)
        return pl.BlockSpec((None, None, SB_HEADS, SB_HEAD_DIM, t), imap)

    grid_spec = pltpu.PrefetchScalarGridSpec(
        num_scalar_prefetch=1,
        grid=(bs, n_pages // n_pg),
        in_specs=[pl.BlockSpec((None, m, SB_WIDTH), lambda b, p, pt: (b, 0, 0)),
                  pl.BlockSpec((m, t), lambda b, p, pt: (0, 0)),
                  pl.BlockSpec((None, SB_WIDTH, t), lambda b, p, pt: (b, 0, 0)),
                  pl.BlockSpec((None, SB_WIDTH, t), lambda b, p, pt: (b, 0, 0)),
                  pl.BlockSpec((t, 2 * t), lambda b, p, pt: (0, 0))]
        + [page_spec(r) for r in range(n_pg)] + [page_spec(r) for r in range(n_pg)],
        out_specs=pl.BlockSpec((None, n_q, SB_WIDTH), lambda b, p, pt: (b, 0, 0)),
        scratch_shapes=[pltpu.VMEM((m, SB_WIDTH), F32), pltpu.VMEM((m, t), F32)],
    )
    return pl.pallas_call(
        functools.partial(_sbs_kernel, n_pg=n_pg, n_q=n_q),
        grid_spec=grid_spec,
        out_shape=jax.ShapeDtypeStruct((bs, n_q, SB_WIDTH), F32),
        compiler_params=_cparams(("arbitrary", "arbitrary")),
        name="sb_sample",
    )(page_table, qm, bias_rows, kn, vn, tri, *([cache_kt] * n_pg), *([cache_vt] * n_pg))


def _causal_conv(xp, w_ref, b_ref, n):
    y = b_ref[...] + w_ref[CONV_WIDTH - 1:CONV_WIDTH, :] * xp[ROW_PAD:]
    for j in range(CONV_WIDTH - 1):
        y = y + w_ref[j:j + 1, :] * pltpu.roll(xp, CONV_WIDTH - 1 - j, 0)[ROW_PAD:]
    return y


def _last_rows(xp, t_valid):
    n = xp.shape[0]
    if t_valid % SUBLANES == 0:
        return xp[t_valid:t_valid + ROW_PAD]
    return pltpu.roll(xp, n - t_valid, 0)[:ROW_PAD]


def _neg_expm1(x):
    p = x * (1.0 + x * (1 / 2 + x * (1 / 6 + x * (1 / 24 + x * (1 / 120 + x * (1 / 720 + x * (1 / 5040)))))))
    return jnp.where(x > -0.25, -p, 1.0 - jnp.exp(x))


def _gelu_tanh(x):
    return 0.5 * x * (1.0 + jnp.tanh(math.sqrt(2.0 / math.pi) * (x + 0.044715 * (x * x * x))))


def _rg_kernel(x_ref, g_ref, conv0_ref, h0_ref, cw_ref, cb_ref, wa_ref, ba_ref, wx_ref, bx_ref, spl_ref,
               y_ref, convn_ref, hn_ref, prev_sc, h_sc, hbuf, *, tt, t_valid, from_start):
    ti = pl.program_id(1)

    @pl.when(ti == 0)
    def _():
        prev_sc[...] = conv0_ref[...]
        h_sc[...] = h0_ref[...]

    x = x_ref[...]
    xp = jnp.concatenate([prev_sc[...], x], axis=0)
    xc = _causal_conv(xp, cw_ref, cb_ref, tt)
    convn_ref[...] = _last_rows(xp, t_valid)
    prev_sc[...] = xp[tt:tt + ROW_PAD]

    xb = xc.astype(BF16)
    r = _sigmoid(jnp.dot(xb, wa_ref[...], preferred_element_type=F32) + ba_ref[...])
    ig = _sigmoid(jnp.dot(xb, wx_ref[...], preferred_element_type=F32) + bx_ref[...])
    log_a = -RG_C * r * spl_ref[...]
    a = jnp.exp(log_a)
    mult = jnp.sqrt(_neg_expm1(2.0 * log_a))
    rowi = lax.broadcasted_iota(jnp.int32, (tt, RG_WIDTH), 0)
    if from_start:
        mult = jnp.where(rowi + ti * tt == 0, 1.0, mult)
    u = xc * ig * mult
    s = 1
    while s < tt:
        keep = rowi >= s
        u = jnp.where(keep, u + a * pltpu.roll(u, s, 0), u)
        a = jnp.where(keep, a * pltpu.roll(a, s, 0), a)
        s *= 2
    h = u + a * h_sc[...]
    hbuf[...] = h
    h_last = hbuf[t_valid - 1:t_valid, :]
    h_sc[...] = h_last
    hn_ref[...] = h_last
    y_ref[...] = (_gelu_tanh(g_ref[...]) * h).astype(y_ref.dtype)


def _rg_call(proj, conv0, h0, cw, cb, wa, ba, wx, bx, spl, *, tt, t_valid, from_start):
    b, t, _ = proj.shape
    assert t % tt == 0 and (t_valid == tt or t == tt)
    w = RG_WIDTH
    vec = pl.BlockSpec((1, w), lambda bi, ti: (0, 0))
    mat = pl.BlockSpec((w, w), lambda bi, ti: (0, 0))
    kern = functools.partial(_rg_kernel, tt=tt, t_valid=t_valid, from_start=from_start)
    return pl.pallas_call(
        kern,
        grid=(b, t // tt),
        in_specs=[pl.BlockSpec((None, tt, w), lambda bi, ti: (bi, ti, C_RGX // w)),
                  pl.BlockSpec((None, tt, w), lambda bi, ti: (bi, ti, C_RGG // w)),
                  pl.BlockSpec((None, ROW_PAD, w), lambda bi, ti: (bi, 0, 0)),
                  pl.BlockSpec((None, 1, w), lambda bi, ti: (bi, 0, 0)),
                  pl.BlockSpec((CONV_WIDTH, w), lambda bi, ti: (0, 0)), vec, mat, vec, mat, vec, vec],
        out_specs=[pl.BlockSpec((None, tt, w), lambda bi, ti: (bi, ti, 0)),
                   pl.BlockSpec((None, ROW_PAD, w), lambda bi, ti: (bi, 0, 0)),
                   pl.BlockSpec((None, 1, w), lambda bi, ti: (bi, 0, 0))],
        out_shape=[jax.ShapeDtypeStruct((b, t, w), BF16),
                   jax.ShapeDtypeStruct((b, ROW_PAD, w), F32),
                   jax.ShapeDtypeStruct((b, 1, w), F32)],
        scratch_shapes=[pltpu.VMEM((ROW_PAD, w), F32), pltpu.VMEM((1, w), F32), pltpu.VMEM((tt, w), F32)],
        compiler_params=_cparams(("arbitrary", "arbitrary")),
        name="rg_lru",
    )(proj, proj, conv0, h0, cw, cb, wa, ba, wx, bx, spl)


def _pad_rows(x, n):
    if x.shape[0] == n:
        return x
    return jnp.concatenate([x, jnp.zeros((n - x.shape[0], x.shape[1]), x.dtype)], axis=0)


def _ssd_kernel(z_ref, xbc_ref, dt_ref, conv0_ref, h0_ref, cw_ref, cb_ref, dtb_ref, a_ref, d_ref, nw_ref,
                y_ref, convn_ref, hn_ref, prev_sc, h_sc, *, l_in, t_valid):
    L = SSD_CHUNK
    N = SSD_STATE
    P2 = 2 * SSD_HEAD_DIM
    ci = pl.program_id(1)

    @pl.when(ci == 0)
    def _():
        prev_sc[...] = conv0_ref[...]
        h_sc[...] = h0_ref[...]

    xbc = _pad_rows(xbc_ref[...], L)
    xp = jnp.concatenate([prev_sc[...], xbc], axis=0)
    xc = _causal_conv(xp, cw_ref, cb_ref, L)
    convn_ref[...] = _last_rows(xp, t_valid)
    prev_sc[...] = xp[L:L + ROW_PAD]
    xc = xc * _sigmoid(xc)
    xs = xc[:, :SSD_INNER]
    bm = xc[:, SSD_INNER:SSD_INNER + SSD_GROUPS * N].astype(BF16)
    cm = xc[:, SSD_INNER + SSD_GROUPS * N:].astype(BF16)

    row = lax.broadcasted_iota(jnp.int32, (L, LANES), 0)
    lane = lax.broadcasted_iota(jnp.int32, (L, LANES), 1)
    dtr = _pad_rows(dt_ref[...], L) + dtb_ref[...]
    dt = jnp.maximum(dtr, 0.0) + jnp.log1p(jnp.exp(-jnp.abs(dtr)))
    dt = jnp.where((lane < SSD_HEADS) & (row < t_valid), dt, 0.0)
    cs = dt * a_ref[...]
    s = 1
    while s < L:
        cs = jnp.where(row >= s, cs + pltpu.roll(cs, s, 0), cs)
        s *= 2
    ecs = jnp.exp(cs)
    cs_t = cs.T
    dt_t = dt.T
    xs_t = xs.T
    tril = row >= lane
    first_head = lane < SSD_HEAD_DIM
    first_rows = row < SSD_HEAD_DIM

    y_pairs = []
    for hp in range(SSD_HEADS // 2):
        g = (2 * hp) // (SSD_HEADS // SSD_GROUPS)
        bm_g = bm[:, g * N:(g + 1) * N]
        cm_g = cm[:, g * N:(g + 1) * N]
        cb = lax.dot_general(cm_g, bm_g, (((1,), (1,)), ((), ())), preferred_element_type=F32)
        x2 = xs[:, hp * P2:(hp + 1) * P2]
        x2b = x2.astype(BF16)
        yd, wrow, cdec, eoff = [], [], [], []
        for hh in range(2):
            h = 2 * hp + hh
            seg = cs[:, h:h + 1] - cs_t[h:h + 1, :]
            decay = jnp.exp(jnp.where(tril, seg, -jnp.inf))
            wts = cb * decay * dt_t[h:h + 1, :]
            yd.append(jnp.dot(wts.astype(BF16), x2b, preferred_element_type=F32))
            last = cs_t[h:h + 1, L - 1:L]
            wrow.append(jnp.exp(last - cs_t[h:h + 1, :]) * dt_t[h:h + 1, :])
            cdec.append(jnp.exp(last))
            eoff.append(ecs[:, h:h + 1])
        h_pair = h_sc[hp * P2:(hp + 1) * P2, :]
        y_off = lax.dot_general(cm_g, h_pair.astype(BF16), (((1,), (1,)), ((), ())),
                                preferred_element_type=F32)
        y_off = y_off * jnp.where(first_head, eoff[0], eoff[1])
        y_pairs.append(jnp.where(first_head, yd[0], yd[1]) + y_off + d_ref[:, hp * P2:(hp + 1) * P2] * x2)
        xw = xs_t[hp * P2:(hp + 1) * P2, :] * jnp.where(first_rows, wrow[0], wrow[1])
        st = jnp.dot(xw.astype(BF16), bm_g, preferred_element_type=F32)
        h_sc[hp * P2:(hp + 1) * P2, :] = jnp.where(first_rows, cdec[0], cdec[1]) * h_pair + st

    hn_ref[...] = h_sc[...]
    y = jnp.concatenate(y_pairs, axis=1)[:l_in]
    zz = z_ref[...]
    u = y * (zz * _sigmoid(zz))
    gw = SSD_INNER // SSD_GROUPS
    u = jnp.concatenate([_rms(u[:, g * gw:(g + 1) * gw]) for g in range(SSD_GROUPS)], axis=1)
    y_ref[...] = (u * nw_ref[...]).astype(y_ref.dtype)


def _ssd_call(proj, conv0, h0, cw, cb, dtb, a_row, d_lane, nw, *, l_in, t_valid):
    b, t, _ = proj.shape
    assert t % l_in == 0 and (l_in == SSD_CHUNK or t == l_in)
    c = SSD_CONV_DIM
    hp_rows = SSD_HEADS * SSD_HEAD_DIM
    kern = functools.partial(_ssd_kernel, l_in=l_in, t_valid=t_valid)
    return pl.pallas_call(
        kern,
        grid=(b, t // l_in),
        in_specs=[pl.BlockSpec((None, l_in, SSD_INNER), lambda bi, ci: (bi, ci, C_Z // SSD_INNER)),
                  pl.BlockSpec((None, l_in, c), lambda bi, ci: (bi, ci, C_XBC // c)),
                  pl.BlockSpec((None, l_in, LANES), lambda bi, ci: (bi, ci, C_DT // LANES)),
                  pl.BlockSpec((None, ROW_PAD, c), lambda bi, ci: (bi, 0, 0)),
                  pl.BlockSpec((None, hp_rows, SSD_STATE), lambda bi, ci: (bi, 0, 0)),
                  pl.BlockSpec((CONV_WIDTH, c), lambda bi, ci: (0, 0)),
                  pl.BlockSpec((1, c), lambda bi, ci: (0, 0)),
                  pl.BlockSpec((1, LANES), lambda bi, ci: (0, 0)),
                  pl.BlockSpec((1, LANES), lambda bi, ci: (0, 0)),
                  pl.BlockSpec((1, SSD_INNER), lambda bi, ci: (0, 0)),
                  pl.BlockSpec((1, SSD_INNER), lambda bi, ci: (0, 0))],
        out_specs=[pl.BlockSpec((None, l_in, SSD_INNER), lambda bi, ci: (bi, ci, 0)),
                   pl.BlockSpec((None, ROW_PAD, c), lambda bi, ci: (bi, 0, 0)),
                   pl.BlockSpec((None, hp_rows, SSD_STATE), lambda bi, ci: (bi, 0, 0))],
        out_shape=[jax.ShapeDtypeStruct((b, t, SSD_INNER), BF16),
                   jax.ShapeDtypeStruct((b, ROW_PAD, c), F32),
                   jax.ShapeDtypeStruct((b, hp_rows, SSD_STATE), F32)],
        scratch_shapes=[pltpu.VMEM((ROW_PAD, c), F32), pltpu.VMEM((hp_rows, SSD_STATE), F32)],
        compiler_params=_cparams(("arbitrary", "arbitrary")),
        name="ssd",
    )(proj, proj, proj, conv0, h0, cw, cb, dtb, a_row, d_lane, nw)


def _merge_kernel(ya_ref, yb_ref, yc_ref, g0_ref, g1_ref, g2_ref, x_ref, gt_ref, sc_ref, sh_ref,
                  wa_ref, wb_ref, wc_ref, wo_ref, nf_ref, wr_ref, br_ref, x1_ref, h2_ref, lg_ref):
    def branch(y_ref, w_ref, g_ref):
        return _sigmoid(g_ref[...]) * jnp.dot(y_ref[...], w_ref[...], preferred_element_type=F32)

    m = branch(ya_ref, wa_ref, g0_ref) + branch(yb_ref, wb_ref, g1_ref) + branch(yc_ref, wc_ref, g2_ref)
    o = jnp.dot(m.astype(BF16), wo_ref[...], preferred_element_type=F32)
    x1 = x_ref[...] + gt_ref[...] * o
    x1_ref[...] = x1
    h2 = _rms(x1) * nf_ref[...] * (1.0 + sc_ref[...]) + sh_ref[...]
    h2_ref[...] = h2.astype(h2_ref.dtype)
    lg_ref[...] = jnp.dot(h2, wr_ref[...], precision=HIGHEST, preferred_element_type=F32) + br_ref[...]


def _merge_call(ya, yb, yc, proj, x, gt, sc, sh, wa, wb, wc, wo, nf, wr, br, *, per_row, rows_per_group, tm):
    r = x.shape[0]
    assert r % tm == 0 and (per_row or rows_per_group % tm == 0)
    d = D_MODEL
    ms = _mod_spec(per_row, tm, rows_per_group)
    yspec = pl.BlockSpec((tm, SB_WIDTH), lambda i: (i, 0))
    gspec = [pl.BlockSpec((tm, d), lambda i, k=k: (i, C_BRG // d + k)) for k in range(N_BRANCH)]
    wspec = pl.BlockSpec((SB_WIDTH, d), lambda i: (0, 0))
    return pl.pallas_call(
        _merge_kernel,
        grid=(r // tm,),
        in_specs=[yspec, yspec, yspec, *gspec, pl.BlockSpec((tm, d), lambda i: (i, 0)), ms, ms, ms,
                  wspec, wspec, wspec, pl.BlockSpec((d, d), lambda i: (0, 0)),
                  pl.BlockSpec((1, d), lambda i: (0, 0)),
                  pl.BlockSpec((d, LANES), lambda i: (0, 0)), pl.BlockSpec((1, LANES), lambda i: (0, 0))],
        out_specs=[pl.BlockSpec((tm, d), lambda i: (i, 0)), pl.BlockSpec((tm, d), lambda i: (i, 0)),
                   pl.BlockSpec((tm, LANES), lambda i: (i, 0))],
        out_shape=[jax.ShapeDtypeStruct((r, d), F32), jax.ShapeDtypeStruct((r, d), BF16),
                   jax.ShapeDtypeStruct((r, LANES), F32)],
        compiler_params=_cparams(("arbitrary",)),
        name="merge",
    )(ya, yb, yc, proj, proj, proj, x, gt, sc, sh, wa, wb, wc, wo, nf, wr, br)


def _moe_kernel(be_ref, xs_ref, *rest):
    ncb = D_MODEL // LANES
    wu_refs = rest[:ncb]
    bg_ref, bl_ref, wd_ref, bd_ref, gate_ref, o_ref, wg_sc, wl_sc, wd_sc = rest[ncb:]
    i = pl.program_id(0)

    @pl.when((i == 0) | (be_ref[i] != be_ref[jnp.maximum(i - 1, 0)]))
    def _():
        rc = MOE_BM
        for c in range(D_FF // rc):
            rows = slice(c * rc, (c + 1) * rc)
            for cb in range(ncb):
                cols = slice(cb * LANES, (cb + 1) * LANES)
                wg_sc[rows, cols] = wu_refs[cb][pl.ds(2 * c * rc, rc, stride=2), :].astype(BF16)
                wl_sc[rows, cols] = wu_refs[cb][pl.ds(2 * c * rc + 1, rc, stride=2), :].astype(BF16)
            wd_sc[rows, :] = wd_ref[rows, :].astype(BF16)

    x = xs_ref[...]
    nt = (((1,), (1,)), ((), ()))
    glu = lax.dot_general(x, wg_sc[...], nt, preferred_element_type=F32) + bg_ref[...]
    lin = lax.dot_general(x, wl_sc[...], nt, preferred_element_type=F32) + bl_ref[...]
    glu = jnp.minimum(glu, SWIGLU_LIMIT)
    lin = jnp.clip(lin, -SWIGLU_LIMIT, SWIGLU_LIMIT)
    act = glu * _sigmoid(SWIGLU_ALPHA * glu) * (lin + 1.0)
    y = jnp.dot(act.astype(BF16), wd_sc[...], preferred_element_type=F32) + bd_ref[...]
    o_ref[...] = y * gate_ref[...]


def _moe_call(layer, block_exp, xs, wu_t, bg, bl, wd, bd, row_gate):
    r = xs.shape[0]
    d = D_MODEL
    assert D_FF % MOE_BM == 0
    bspec = pl.BlockSpec((None, None, 1, d), lambda i, be: (layer, be[i], 0, 0))
    ncb = d // LANES
    wu_specs = [pl.BlockSpec((None, None, 2 * D_FF, LANES), lambda i, be, cb=cb: (layer, be[i], 0, cb))
                for cb in range(ncb)]
    grid_spec = pltpu.PrefetchScalarGridSpec(
        num_scalar_prefetch=1,
        grid=(r // MOE_BM,),
        in_specs=[pl.BlockSpec((MOE_BM, d), lambda i, be: (i, 0)), *wu_specs, bspec, bspec,
                  pl.BlockSpec((None, None, D_FF, d), lambda i, be: (layer, be[i], 0, 0)), bspec,
                  pl.BlockSpec((MOE_BM, 1), lambda i, be: (i, 0))],
        out_specs=pl.BlockSpec((MOE_BM, d), lambda i, be: (i, 0)),
        scratch_shapes=[pltpu.VMEM((D_FF, d), BF16), pltpu.VMEM((D_FF, d), BF16), pltpu.VMEM((D_FF, d), BF16)],
    )
    return pl.pallas_call(
        _moe_kernel,
        grid_spec=grid_spec,
        out_shape=jax.ShapeDtypeStruct((r, d), F32),
        compiler_params=_cparams(("arbitrary",), MOE_VMEM_LIMIT),
        name="moe",
    )(block_exp, xs, *([wu_t] * ncb), bg, bl, wd, bd, row_gate)


def _route(logits, bm):
    n_tok = logits.shape[0]
    n_slot = n_tok * TOP_K
    n_blocks = -(-(n_slot + N_EXPERTS * (bm - 1)) // bm)
    n_rows = n_blocks * bm
    top_val, top_idx = lax.top_k(logits, TOP_K)
    gate = jax.nn.softmax(top_val, axis=-1).reshape(n_slot)
    e_flat = top_idx.reshape(n_slot).astype(jnp.int32)
    order = jnp.argsort(e_flat, stable=True).astype(jnp.int32)
    rank = jnp.argsort(order).astype(jnp.int32)
    experts = jnp.arange(N_EXPERTS, dtype=jnp.int32)
    e_lanes = jnp.pad(e_flat, (0, (-n_slot) % LANES), constant_values=-1).reshape(-1, LANES)
    counts = jnp.sum((e_lanes[None] == experts[:, None, None]).astype(jnp.int32), axis=(1, 2))
    padded = (counts + bm - 1) // bm * bm
    start = jnp.cumsum(counts) - counts
    pend = jnp.cumsum(padded)
    shift = (pend - padded) - start
    block_row = jnp.arange(n_blocks, dtype=jnp.int32) * bm
    block_exp = jnp.minimum(jnp.sum((pend[None, :] <= block_row[:, None]).astype(jnp.int32), axis=1),
                            N_EXPERTS - 1)
    rows = jnp.arange(n_rows, dtype=jnp.int32)
    row_exp = jnp.repeat(block_exp, bm)
    valid = (rows - (pend - padded)[row_exp]) < counts[row_exp]
    valid = valid & (rows < pend[N_EXPERTS - 1])
    row_slot = order[jnp.clip(rows - shift[row_exp], 0, n_slot - 1)]
    row_tok = jnp.where(valid, row_slot // TOP_K, 0)
    row_gate = jnp.where(valid, gate[row_slot], 0.0)
    slot_row = rank + shift[e_flat]
    return block_exp, row_tok, row_gate, slot_row.reshape(n_tok, TOP_K)


def _resid_kernel(x_ref, f_ref, gt_ref, nw_ref, o_ref, *, final):
    x = x_ref[...] + gt_ref[...] * f_ref[...]
    if final:
        x = _rms(x) * nw_ref[...]
    o_ref[...] = x


def _resid_call(x, f, gt, nw, *, final, per_row, rows_per_group, tm):
    r = x.shape[0]
    d = D_MODEL
    row = pl.BlockSpec((tm, d), lambda i: (i, 0))
    return pl.pallas_call(
        functools.partial(_resid_kernel, final=final),
        grid=(r // tm,),
        in_specs=[row, row, _mod_spec(per_row, tm, rows_per_group), pl.BlockSpec((1, d), lambda i: (0, 0))],
        out_specs=row,
        out_shape=jax.ShapeDtypeStruct((r, d), F32),
        compiler_params=_cparams(("arbitrary",)),
        name="resid",
    )(x, f, gt, nw)


def _block_diag(w):
    depth, nb, k, _ = w.shape
    eye = jnp.eye(nb, dtype=w.dtype)
    return jnp.einsum('lnij,nm->lnimj', w, eye).reshape(depth, nb * k, nb * k)


def _pick_tile(n, pref):
    t = min(pref, n)
    while n % t:
        t //= 2
    return t


def kernel(x_prompt, x_sample, cache_k, cache_v, state_rg_conv, state_rg_h, state_ssd_conv, state_ssd, page_table, c_prompt, c_sample, w_mod, b_mod, norm_mix, norm_ffn, norm_final, w_in, sb_bias, rg_conv_w, rg_conv_b, rg_w_a, rg_b_a, rg_w_x, rg_b_x, rg_lam, ssd_conv_w, ssd_conv_b, ssd_dt_bias, ssd_a_log, ssd_d, ssd_norm, w_branch, w_out, w_router, b_router, w_up, b_up, w_down, b_down):
    depth = w_in.shape[0]
    b, s, d = x_prompt.shape
    bs, ts, _ = x_sample.shape
    tp = ROW_PAD
    n_p, n_s = b * s, bs * tp
    past_len = page_table.shape[1] * cache_k.shape[2]

    w_in_p = jnp.concatenate(
        [w_in[:, :, :4096], w_in[:, :, 4104:], w_in[:, :, 4096:4104],
         jnp.zeros((depth, d, NP_COLS - w_in.shape[2]), w_in.dtype)], axis=2).astype(BF16)
    wb = w_branch.astype(BF16)
    wb_a, wb_b, wb_c = wb[:, :SB_WIDTH], wb[:, SB_WIDTH:SB_WIDTH + RG_WIDTH], wb[:, SB_WIDTH + RG_WIDTH:]
    wo = w_out.astype(BF16)
    wr = jnp.pad(w_router, ((0, 0), (0, 0), (0, LANES - N_EXPERTS)))
    br = jnp.pad(b_router, ((0, 0), (0, LANES - N_EXPERTS))).reshape(depth, 1, LANES)
    wu_t = jnp.swapaxes(w_up, 2, 3)
    bg = b_up[..., 0::2].reshape(depth, N_EXPERTS, 1, D_FF)
    bl = b_up[..., 1::2].reshape(depth, N_EXPERTS, 1, D_FF)
    bd = b_down.reshape(depth, N_EXPERTS, 1, d)
    rg_wa = _block_diag(rg_w_a).astype(BF16)
    rg_wx = _block_diag(rg_w_x).astype(BF16)
    rg_spl = jax.nn.softplus(-rg_lam)
    ssd_a = jnp.pad(-jnp.exp(ssd_a_log), ((0, 0), (0, LANES - SSD_HEADS)))
    ssd_dtb = jnp.pad(ssd_dt_bias, ((0, 0), (0, LANES - SSD_HEADS)))
    ssd_d_lane = jnp.repeat(ssd_d, SSD_HEAD_DIM, axis=1)
    tri = _tri_matrix(KEY_TILE)
    cache_kt = jnp.transpose(cache_k, (0, 1, 3, 4, 2))
    cache_vt = jnp.transpose(cache_v, (0, 1, 3, 4, 2))

    mod = _mod_call(jnp.concatenate([c_prompt, c_sample], axis=0), w_mod, b_mod)

    def row2(v):
        return v.reshape(1, -1)

    def pad_prev(st):
        return jnp.pad(st, ((0, 0), (tp - (CONV_WIDTH - 1), 0), (0, 0)))

    xp = x_prompt.reshape(n_p, d)
    xq = jnp.pad(x_sample, ((0, 0), (0, tp - ts), (0, 0))).reshape(n_s, d)
    tm_in = _pick_tile(s, 1024)
    tm_mg = _pick_tile(s, 256)
    tt_rg = _pick_tile(s, 256)
    tm_rs = _pick_tile(s, 512)
    outs_p, outs_s = [], []
    for l in range(depth):
        mp = mod[l, :b].reshape(b, 1, 6, d)
        mq = jnp.repeat(mod[l, b:], tp, axis=0).reshape(n_s, 6, d)
        sh1p, sc1p, gt1p, sh2p, sc2p, gt2p = (mp[:, :, k] for k in range(6))
        sh1q, sc1q, gt1q, sh2q, sc2q, gt2q = (mq[:, k] for k in range(6))
        bias_l = sb_bias[l]

        proj_p = _inproj_call(xp, sc1p, sh1p, row2(norm_mix[l]), w_in_p[l],
                              per_row=False, rows_per_group=s, tm=tm_in).reshape(b, s, NP_COLS)
        ya_p = _sbp_call(proj_p, bias_l, tri)
        yb_p, rgc_p, rgh_p = _rg_call(
            proj_p, jnp.zeros((b, tp, RG_WIDTH), F32), jnp.zeros((b, 1, RG_WIDTH), F32),
            rg_conv_w[l], row2(rg_conv_b[l]), rg_wa[l], row2(rg_b_a[l]), rg_wx[l], row2(rg_b_x[l]),
            row2(rg_spl[l]), tt=tt_rg, t_valid=tt_rg, from_start=True)
        yc_p, ssc_p, ss_p = _ssd_call(
            proj_p, jnp.zeros((b, tp, SSD_CONV_DIM), F32),
            jnp.zeros((b, SSD_HEADS * SSD_HEAD_DIM, SSD_STATE), F32),
            ssd_conv_w[l], row2(ssd_conv_b[l]), row2(ssd_dtb[l]), row2(ssd_a[l]), row2(ssd_d_lane[l]),
            row2(ssd_norm[l]), l_in=SSD_CHUNK, t_valid=SSD_CHUNK)
        x1p, h2p, lgp = _merge_call(
            ya_p.reshape(n_p, -1), yb_p.reshape(n_p, -1), yc_p.reshape(n_p, -1), proj_p.reshape(n_p, NP_COLS),
            xp, gt1p, sc2p, sh2p, wb_a[l], wb_b[l], wb_c[l], wo[l], row2(norm_ffn[l]), wr[l], br[l],
            per_row=False, rows_per_group=s, tm=tm_mg)
        outs_p.append((proj_p[:, :, C_K:C_K + SB_WIDTH], proj_p[:, :, C_V:C_V + SB_WIDTH],
                       rgc_p[:, tp - 3:], rgh_p[:, 0], ssc_p[:, tp - 3:], ss_p))

        proj_q = _inproj_call(xq, sc1q, sh1q, row2(norm_mix[l]), w_in_p[l],
                              per_row=True, rows_per_group=tp, tm=n_s).reshape(bs, tp, NP_COLS)
        q_new = proj_q[:, :ts, C_Q:C_Q + SB_WIDTH] * (SB_HEAD_DIM ** -0.5)
        k_new = proj_q[:, :ts, C_K:C_K + SB_WIDTH]
        v_new = proj_q[:, :ts, C_V:C_V + SB_WIDTH]
        head_of_col = jnp.arange(SB_WIDTH) // SB_HEAD_DIM
        own = head_of_col[None, :] == jnp.arange(SB_HEADS)[:, None]
        qm = jnp.where(own[None, None], q_new[:, :, None, :], 0.0).reshape(bs, ts * SB_HEADS, SB_WIDTH)
        bias_rows = jnp.broadcast_to(jnp.tile(bias_l, ts)[:, None], (ts * SB_HEADS, KEY_TILE))
        kn = jnp.pad(jnp.swapaxes(k_new, 1, 2), ((0, 0), (0, 0), (0, KEY_TILE - ts))).astype(BF16)
        vn = jnp.pad(jnp.swapaxes(v_new, 1, 2), ((0, 0), (0, 0), (0, KEY_TILE - ts))).astype(BF16)
        ya_q = _sbs_call(l, page_table, qm.astype(BF16), bias_rows, kn, vn, tri, cache_kt, cache_vt, ts)
        ya_q = jnp.pad(ya_q, ((0, 0), (0, tp - ts), (0, 0))).astype(BF16)
        yb_q, rgc_q, rgh_q = _rg_call(
            proj_q, pad_prev(state_rg_conv[l]), state_rg_h[l][:, None, :],
            rg_conv_w[l], row2(rg_conv_b[l]), rg_wa[l], row2(rg_b_a[l]), rg_wx[l], row2(rg_b_x[l]),
            row2(rg_spl[l]), tt=tp, t_valid=ts, from_start=(past_len == 0))
        yc_q, ssc_q, ss_q = _ssd_call(
            proj_q, pad_prev(state_ssd_conv[l]),
            state_ssd[l].reshape(bs, SSD_HEADS * SSD_HEAD_DIM, SSD_STATE),
            ssd_conv_w[l], row2(ssd_conv_b[l]), row2(ssd_dtb[l]), row2(ssd_a[l]), row2(ssd_d_lane[l]),
            row2(ssd_norm[l]), l_in=tp, t_valid=ts)
        x1q, h2q, lgq = _merge_call(
            ya_q.reshape(n_s, -1), yb_q.reshape(n_s, -1), yc_q.reshape(n_s, -1), proj_q.reshape(n_s, NP_COLS),
            xq, gt1q, sc2q, sh2q, wb_a[l], wb_b[l], wb_c[l], wo[l], row2(norm_ffn[l]), wr[l], br[l],
            per_row=True, rows_per_group=tp, tm=n_s)
        outs_s.append((k_new, v_new, rgc_q[:, tp - 3:], rgh_q[:, 0], ssc_q[:, tp - 3:], ss_q))

        h2 = jnp.concatenate([h2p, h2q], axis=0)
        logits = jnp.concatenate([lgp, lgq], axis=0)[:, :N_EXPERTS]
        block_exp, row_tok, row_gate, slot_row = _route(logits, MOE_BM)
        ys = _moe_call(l, block_exp, h2[row_tok], wu_t, bg, bl, w_down, bd, row_gate[:, None])
        f = jnp.sum(ys[slot_row], axis=1)
        final = l == depth - 1
        xp = _resid_call(x1p, f[:n_p], gt2p, row2(norm_final), final=final,
                         per_row=False, rows_per_group=s, tm=tm_rs)
        xq = _resid_call(x1q, f[n_p:], gt2q, row2(norm_final), final=final,
                         per_row=True, rows_per_group=tp, tm=n_s)

    def stack(outs, i, shape):
        return jnp.stack([o[i] for o in outs]).reshape(shape)

    y_prompt = xp.reshape(b, s, d)
    y_sample = xq.reshape(bs, tp, d)[:, :ts]
    hs = (SSD_HEADS, SSD_HEAD_DIM, SSD_STATE)
    return (y_prompt, y_sample,
            stack(outs_p, 0, (depth, b, s, SB_HEADS, SB_HEAD_DIM)),
            stack(outs_p, 1, (depth, b, s, SB_HEADS, SB_HEAD_DIM)),
            stack(outs_p, 2, (depth, b, CONV_WIDTH - 1, RG_WIDTH)), stack(outs_p, 3, (depth, b, RG_WIDTH)),
            stack(outs_p, 4, (depth, b, CONV_WIDTH - 1, SSD_CONV_DIM)), stack(outs_p, 5, (depth, b) + hs),
            stack(outs_s, 0, (depth, bs, ts, SB_HEADS, SB_HEAD_DIM)),
            stack(outs_s, 1, (depth, bs, ts, SB_HEADS, SB_HEAD_DIM)),
            stack(outs_s, 2, (depth, bs, CONV_WIDTH - 1, RG_WIDTH)), stack(outs_s, 3, (depth, bs, RG_WIDTH)),
            stack(outs_s, 4, (depth, bs, CONV_WIDTH - 1, SSD_CONV_DIM)), stack(outs_s, 5, (depth, bs) + hs))
```

```python
import functools
import math

import jax
import jax.numpy as jnp
from jax import lax
from jax.experimental import pallas as pl
from jax.experimental.pallas import tpu as pltpu

F32 = jnp.float32
BF16 = jnp.bfloat16
HIGHEST = lax.Precision.HIGHEST

D_MODEL = 1024
SB_HEADS = 8
SB_HEAD_DIM = 64
SB_WIDTH = SB_HEADS * SB_HEAD_DIM
RG_WIDTH = 512
RG_BLOCKS = 8
RG_C = 8.0
CONV_WIDTH = 4
SSD_INNER = 512
SSD_HEAD_DIM = 64
SSD_HEADS = SSD_INNER // SSD_HEAD_DIM
SSD_GROUPS = 2
SSD_STATE = 128
SSD_CHUNK = 128
SSD_CONV_DIM = SSD_INNER + 2 * SSD_GROUPS * SSD_STATE
N_BRANCH = 3
N_EXPERTS = 32
TOP_K = 4
D_FF = D_MODEL
SWIGLU_LIMIT = 7.0
SWIGLU_ALPHA = 1.702
EPS = 1e-6

SUBLANES = 8
LANES = 128
VMEM_LIMIT = 48 * 1024 * 1024

C_Q, C_K, C_V, C_RGX, C_RGG, C_Z, C_XBC, C_BRG, C_DT = 0, 512, 1024, 1536, 2048, 2560, 3072, 4096, 7168
NP_COLS = 7680
IN_TN = 1536
ROW_PAD = 8
KEY_TILE = 128
MOE_BM = 256
PAGES_PER_STEP = 16
SBP_SUBTILES = 8
MOE_VMEM_LIMIT = 56 * 1024 * 1024


def _cparams(sem, vmem=VMEM_LIMIT):
    return pltpu.CompilerParams(dimension_semantics=sem, vmem_limit_bytes=vmem)


def _sigmoid(x):
    return 1.0 / (1.0 + jnp.exp(-x))


def _rms(x):
    return x * lax.rsqrt(jnp.mean(x * x, axis=-1, keepdims=True) + EPS)


def _mod_kernel(c_ref, w_ref, b_ref, o_ref):
    c = c_ref[...]
    s = c * _sigmoid(c)
    o_ref[...] = jnp.dot(s, w_ref[...], precision=HIGHEST, preferred_element_type=F32) + b_ref[...]


def _mod_call(c_all, w_mod, b_mod):
    depth, d, n = w_mod.shape
    nc = c_all.shape[0]
    tn = 1536
    return pl.pallas_call(
        _mod_kernel,
        grid=(depth, n // tn),
        in_specs=[pl.BlockSpec((nc, d), lambda l, j: (0, 0)),
                  pl.BlockSpec((None, d, tn), lambda l, j: (l, 0, j)),
                  pl.BlockSpec((None, 1, tn), lambda l, j: (l, 0, j))],
        out_specs=pl.BlockSpec((None, nc, tn), lambda l, j: (l, 0, j)),
        out_shape=jax.ShapeDtypeStruct((depth, nc, n), F32),
        compiler_params=_cparams(("arbitrary", "arbitrary")),
        name="mod",
    )(c_all, w_mod, b_mod.reshape(depth, 1, n))


def _inproj_kernel(x_ref, sc_ref, sh_ref, g_ref, w_ref, o_ref, h_ref):
    @pl.when(pl.program_id(1) == 0)
    def _():
        h = _rms(x_ref[...]) * g_ref[...]
        h_ref[...] = (h * (1.0 + sc_ref[...]) + sh_ref[...]).astype(BF16)

    o_ref[...] = jnp.dot(h_ref[...], w_ref[...], preferred_element_type=F32)


def _mod_spec(per_row, tm, rows_per_group):
    if per_row:
        return pl.BlockSpec((tm, D_MODEL), lambda i, *_: (i, 0))
    return pl.BlockSpec((None, 1, D_MODEL), lambda i, *_: ((i * tm) // rows_per_group, 0, 0))


def _inproj_call(x, sc, sh, g, w, *, per_row, rows_per_group, tm):
    r = x.shape[0]
    assert r % tm == 0 and (per_row or rows_per_group % tm == 0)
    ms = _mod_spec(per_row, tm, rows_per_group)
    return pl.pallas_call(
        _inproj_kernel,
        grid=(r // tm, NP_COLS // IN_TN),
        in_specs=[pl.BlockSpec((tm, D_MODEL), lambda i, j: (i, 0)), ms, ms,
                  pl.BlockSpec((1, D_MODEL), lambda i, j: (0, 0)),
                  pl.BlockSpec((D_MODEL, IN_TN), lambda i, j: (0, j))],
        out_specs=pl.BlockSpec((tm, IN_TN), lambda i, j: (i, j)),
        out_shape=jax.ShapeDtypeStruct((r, NP_COLS), F32),
        scratch_shapes=[pltpu.VMEM((tm, D_MODEL), BF16)],
        compiler_params=_cparams(("arbitrary", "arbitrary")),
        name="inproj",
    )(x, sc, sh, g, w)


def _sb_logs(z, mask=None):
    sp = jnp.log(1.0 + jnp.exp(-jnp.abs(z)))
    m = jnp.minimum(z, 0.0)
    lk = (m - z) - sp
    if mask is not None:
        lk = jnp.where(mask, lk, 0.0)
    return m - sp, lk


def _sb_mass(lk, tri):
    return jnp.dot(lk.astype(BF16), tri, preferred_element_type=F32)


def _sb_tile(z, tri, c_old, mask):
    t = z.shape[1]
    ls, lk = _sb_logs(z, mask)
    r = _sb_mass(lk, tri)
    w = jnp.exp(ls + r[:, :t] + c_old)
    if mask is not None:
        w = jnp.where(mask, w, 0.0)
    return w, c_old + r[:, t:]


def _tri_matrix(t):
    j = lax.broadcasted_iota(jnp.int32, (t, 2 * t), 0)
    s = lax.broadcasted_iota(jnp.int32, (t, 2 * t), 1)
    return jnp.where((j > s) | (s >= t), 1.0, 0.0).astype(BF16)


def _sbp_kernel(bias_ref, q_ref, k_ref, v_ref, tri_ref, o_ref, kb, vb, qm, acc, cc, *, scale, nq):
    t = KEY_TILE
    hp = pl.program_id(1)
    i = pl.program_id(2)

    @pl.when(i == 0)
    def _():
        kb[...] = k_ref[...].astype(BF16)
        vb[...] = v_ref[...].astype(BF16)

    lane = lax.broadcasted_iota(jnp.int32, (nq * t, LANES), 1)
    first_head = lane < SB_HEAD_DIM
    q = q_ref[...] * scale
    qm[0] = jnp.where(first_head, q, 0.0).astype(BF16)
    qm[1] = jnp.where(first_head, 0.0, q).astype(BF16)
    cc[...] = jnp.zeros_like(cc)
    acc[...] = jnp.zeros_like(acc)
    tri = tri_ref[...]
    row = lax.broadcasted_iota(jnp.int32, (t, t), 0)
    col = lax.broadcasted_iota(jnp.int32, (t, t), 1)
    causal = col < row
    bias = (bias_ref[2 * hp], bias_ref[2 * hp + 1])

    def tile(j, subs):
        start = pl.multiple_of(j * t, t)
        kt = kb[pl.ds(start, t), :]
        vt = vb[pl.ds(start, t), :]
        chains = [(hh, slice(r * t, (r + 1) * t), causal if masked else None)
                  for r, masked in subs for hh in range(2)]
        zs = [lax.dot_general(qm[hh, rows], kt, (((1,), (1,)), ((), ())), preferred_element_type=F32) + bias[hh]
              for hh, rows, _ in chains]
        logs = [_sb_logs(z, mask) for z, (_, _, mask) in zip(zs, chains)]
        mass = [_sb_mass(lk, tri) for _, lk in logs]
        pvs = []
        for (hh, rows, mask), (ls, _), r in zip(chains, logs, mass):
            c_old = cc[hh, rows]
            w = jnp.exp(ls + r[:, :t] + c_old)
            if mask is not None:
                w = jnp.where(mask, w, 0.0)
            cc[hh, rows] = c_old + r[:, t:]
            pvs.append(jnp.dot(w.astype(BF16), vt, preferred_element_type=F32))
        for (hh, rows, _), pv in zip(chains, pvs):
            acc[hh, rows] += pv

    for c in range(nq - 1, -1, -1):
        tile(i * nq + c, [(r, r == c) for r in range(c, nq)])

    def body(jj, carry):
        tile(i * nq - 1 - jj, [(r, False) for r in range(nq)])
        return carry

    lax.fori_loop(0, i * nq, body, 0)
    o_ref[...] = jnp.where(first_head, acc[0], acc[1]).astype(o_ref.dtype)


def _sbp_call(proj, bias, tri):
    b, s, _ = proj.shape
    t = KEY_TILE
    nq = _pick_tile(s // t, SBP_SUBTILES)
    tq = nq * t
    assert s % tq == 0
    kern = functools.partial(_sbp_kernel, scale=SB_HEAD_DIM ** -0.5, nq=nq)
    return pl.pallas_call(
        kern,
        grid=(b, SB_WIDTH // LANES, s // tq),
        in_specs=[pl.BlockSpec(memory_space=pltpu.SMEM),
                  pl.BlockSpec((None, tq, LANES), lambda bi, hp, i: (bi, i, C_Q // LANES + hp)),
                  pl.BlockSpec((None, s, LANES), lambda bi, hp, i: (bi, 0, C_K // LANES + hp)),
                  pl.BlockSpec((None, s, LANES), lambda bi, hp, i: (bi, 0, C_V // LANES + hp)),
                  pl.BlockSpec((t, 2 * t), lambda bi, hp, i: (0, 0))],
        out_specs=pl.BlockSpec((None, tq, LANES), lambda bi, hp, i: (bi, i, hp)),
        out_shape=jax.ShapeDtypeStruct((b, s, SB_WIDTH), BF16),
        scratch_shapes=[pltpu.VMEM((s, LANES), BF16), pltpu.VMEM((s, LANES), BF16),
                        pltpu.VMEM((2, tq, LANES), BF16),
                        pltpu.VMEM((2, tq, LANES), F32), pltpu.VMEM((2, tq, LANES), F32)],
        compiler_params=_cparams(("arbitrary", "arbitrary", "arbitrary")),
        name="sb_prompt",
    )(bias, proj, proj, proj, tri)


def _sbs_kernel(pt_ref, qm_ref, bias_ref, kn_ref, vn_ref, tri_ref, *rest, n_pg, n_q):
    del pt_ref
    k_refs = rest[:n_pg]
    v_refs = rest[n_pg:2 * n_pg]
    o_ref, acc, cc = rest[2 * n_pg:]
    t = KEY_TILE
    m = n_q * SB_HEADS
    p = pl.program_id(1)
    qm = qm_ref[...]
    bias = bias_ref[...]
    tri = tri_ref[...]

    def pv(w, vt):
        return lax.dot_general(w.astype(BF16), vt, (((1,), (1,)), ((), ())), preferred_element_type=F32)

    @pl.when(p == 0)
    def _():
        row = lax.broadcasted_iota(jnp.int32, (m, t), 0)
        col = lax.broadcasted_iota(jnp.int32, (m, t), 1)
        z = jnp.dot(qm, kn_ref[...], preferred_element_type=F32) + bias
        w, c_new = _sb_tile(z, tri, jnp.zeros((m, t), F32), col < row // SB_HEADS)
        cc[...] = c_new
        acc[...] = pv(w, vn_ref[...])

    logs = [_sb_logs(jnp.dot(qm, k_refs[r][...].reshape(SB_WIDTH, t).astype(BF16),
                             preferred_element_type=F32) + bias) for r in range(n_pg)]
    mass = [_sb_mass(lk, tri) for _, lk in logs]
    c = cc[...]
    out = acc[...]
    for r in range(n_pg):
        w = jnp.exp(logs[r][0] + mass[r][:, :t] + c)
        c = c + mass[r][:, t:]
        out = out + pv(w, v_refs[r][...].reshape(SB_WIDTH, t).astype(BF16))
    cc[...] = c
    acc[...] = out

    @pl.when(p == pl.num_programs(1) - 1)
    def _():
        a = acc[...]
        row = lax.broadcasted_iota(jnp.int32, (m, SB_WIDTH), 0)
        lane = lax.broadcasted_iota(jnp.int32, (m, SB_WIDTH), 1)
        own = (lane // SB_HEAD_DIM) == (row % SB_HEADS)
        o_ref[...] = jnp.sum(jnp.where(own, a, 0.0).reshape(n_q, SB_HEADS, SB_WIDTH), axis=1)


def _sbs_call(layer, page_table, qm, bias_rows, kn, vn, tri, cache_kt, cache_vt, n_q):
    bs, n_pages = page_table.shape
    n_pg = min(PAGES_PER_STEP, n_pages)
    assert n_pages % n_pg == 0
    m = n_q * SB_HEADS
    t = KEY_TILE
    assert cache_kt.shape[-1] == t

    def page_spec(r):
        def imap(b, p, pt):
            return (layer, pt[b, n_pages - 1 - (p * n_pg + r)], 0, 0, 0)
        return pl.BlockSpec((None, None, SB_HEADS, SB_HEAD_DIM, t), imap)

    grid_spec = pltpu.PrefetchScalarGridSpec(
        num_scalar_prefetch=1,
        grid=(bs, n_pages // n_pg),
        in_specs=[pl.BlockSpec((None, m, SB_WIDTH), lambda b, p, pt: (b, 0, 0)),
                  pl.BlockSpec((m, t), lambda b, p, pt: (0, 0)),
                  pl.BlockSpec((None, SB_WIDTH, t), lambda b, p, pt: (b, 0, 0)),
                  pl.BlockSpec((None, SB_WIDTH, t), lambda b, p, pt: (b, 0, 0)),
                  pl.BlockSpec((t, 2 * t), lambda b, p, pt: (0, 0))]
        + [page_spec(r) for r in range(n_pg)] + [page_spec(r) for r in range(n_pg)],
        out_specs=pl.BlockSpec((None, n_q, SB_WIDTH), lambda b, p, pt: (b, 0, 0)),
        scratch_shapes=[pltpu.VMEM((m, SB_WIDTH), F32), pltpu.VMEM((m, t), F32)],
    )
    return pl.pallas_call(
        functools.partial(_sbs_kernel, n_pg=n_pg, n_q=n_q),
        grid_spec=grid_spec,
        out_shape=jax.ShapeDtypeStruct((bs, n_q, SB_WIDTH), F32),
        compiler_params=_cparams(("arbitrary", "arbitrary")),
        name="sb_sample",
    )(page_table, qm, bias_rows, kn, vn, tri, *([cache_kt] * n_pg), *([cache_vt] * n_pg))


def _causal_conv(xp, w_ref, b_ref, n):
    y = b_ref[...] + w_ref[CONV_WIDTH - 1:CONV_WIDTH, :] * xp[ROW_PAD:]
    for j in range(CONV_WIDTH - 1):
        y = y + w_ref[j:j + 1, :] * pltpu.roll(xp, CONV_WIDTH - 1 - j, 0)[ROW_PAD:]
    return y


def _last_rows(xp, t_valid):
    n = xp.shape[0]
    if t_valid % SUBLANES == 0:
        return xp[t_valid:t_valid + ROW_PAD]
    return pltpu.roll(xp, n - t_valid, 0)[:ROW_PAD]


def _neg_expm1(x):
    p = x * (1.0 + x * (1 / 2 + x * (1 / 6 + x * (1 / 24 + x * (1 / 120 + x * (1 / 720 + x * (1 / 5040)))))))
    return jnp.where(x > -0.25, -p, 1.0 - jnp.exp(x))


def _gelu_tanh(x):
    return 0.5 * x * (1.0 + jnp.tanh(math.sqrt(2.0 / math.pi) * (x + 0.044715 * (x * x * x))))


def _rg_kernel(x_ref, g_ref, conv0_ref, h0_ref, cw_ref, cb_ref, wa_ref, ba_ref, wx_ref, bx_ref, spl_ref,
               y_ref, convn_ref, hn_ref, prev_sc, h_sc, hbuf, *, tt, t_valid, from_start):
    ti = pl.program_id(1)

    @pl.when(ti == 0)
    def _():
        prev_sc[...] = conv0_ref[...]
        h_sc[...] = h0_ref[...]

    x = x_ref[...]
    xp = jnp.concatenate([prev_sc[...], x], axis=0)
    xc = _causal_conv(xp, cw_ref, cb_ref, tt)
    convn_ref[...] = _last_rows(xp, t_valid)
    prev_sc[...] = xp[tt:tt + ROW_PAD]

    xb = xc.astype(BF16)
    r = _sigmoid(jnp.dot(xb, wa_ref[...], preferred_element_type=F32) + ba_ref[...])
    ig = _sigmoid(jnp.dot(xb, wx_ref[...], preferred_element_type=F32) + bx_ref[...])
    log_a = -RG_C * r * spl_ref[...]
    a = jnp.exp(log_a)
    mult = jnp.sqrt(_neg_expm1(2.0 * log_a))
    rowi = lax.broadcasted_iota(jnp.int32, (tt, RG_WIDTH), 0)
    if from_start:
        mult = jnp.where(rowi + ti * tt == 0, 1.0, mult)
    u = xc * ig * mult
    s = 1
    while s < tt:
        keep = rowi >= s
        u = jnp.where(keep, u + a * pltpu.roll(u, s, 0), u)
        a = jnp.where(keep, a * pltpu.roll(a, s, 0), a)
        s *= 2
    h = u + a * h_sc[...]
    hbuf[...] = h
    h_last = hbuf[t_valid - 1:t_valid, :]
    h_sc[...] = h_last
    hn_ref[...] = h_last
    y_ref[...] = (_gelu_tanh(g_ref[...]) * h).astype(y_ref.dtype)


def _rg_call(proj, conv0, h0, cw, cb, wa, ba, wx, bx, spl, *, tt, t_valid, from_start):
    b, t, _ = proj.shape
    assert t % tt == 0 and (t_valid == tt or t == tt)
    w = RG_WIDTH
    vec = pl.BlockSpec((1, w), lambda bi, ti: (0, 0))
    mat = pl.BlockSpec((w, w), lambda bi, ti: (0, 0))
    kern = functools.partial(_rg_kernel, tt=tt, t_valid=t_valid, from_start=from_start)
    return pl.pallas_call(
        kern,
        grid=(b, t // tt),
        in_specs=[pl.BlockSpec((None, tt, w), lambda bi, ti: (bi, ti, C_RGX // w)),
                  pl.BlockSpec((None, tt, w), lambda bi, ti: (bi, ti, C_RGG // w)),
                  pl.BlockSpec((None, ROW_PAD, w), lambda bi, ti: (bi, 0, 0)),
                  pl.BlockSpec((None, 1, w), lambda bi, ti: (bi, 0, 0)),
                  pl.BlockSpec((CONV_WIDTH, w), lambda bi, ti: (0, 0)), vec, mat, vec, mat, vec, vec],
        out_specs=[pl.BlockSpec((None, tt, w), lambda bi, ti: (bi, ti, 0)),
                   pl.BlockSpec((None, ROW_PAD, w), lambda bi, ti: (bi, 0, 0)),
                   pl.BlockSpec((None, 1, w), lambda bi, ti: (bi, 0, 0))],
        out_shape=[jax.ShapeDtypeStruct((b, t, w), BF16),
                   jax.ShapeDtypeStruct((b, ROW_PAD, w), F32),
                   jax.ShapeDtypeStruct((b, 1, w), F32)],
        scratch_shapes=[pltpu.VMEM((ROW_PAD, w), F32), pltpu.VMEM((1, w), F32), pltpu.VMEM((tt, w), F32)],
        compiler_params=_cparams(("arbitrary", "arbitrary")),
        name="rg_lru",
    )(proj, proj, conv0, h0, cw, cb, wa, ba, wx, bx, spl)


def _pad_rows(x, n):
    if x.shape[0] == n:
        return x
    return jnp.concatenate([x, jnp.zeros((n - x.shape[0], x.shape[1]), x.dtype)], axis=0)


def _ssd_kernel(z_ref, xbc_ref, dt_ref, conv0_ref, h0_ref, cw_ref, cb_ref, dtb_ref, a_ref, d_ref, nw_ref,
                y_ref, convn_ref, hn_ref, prev_sc, h_sc, *, l_in, t_valid):
    L = SSD_CHUNK
    N = SSD_STATE
    P2 = 2 * SSD_HEAD_DIM
    ci = pl.program_id(1)

    @pl.when(ci == 0)
    def _():
        prev_sc[...] = conv0_ref[...]
        h_sc[...] = h0_ref[...]

    xbc = _pad_rows(xbc_ref[...], L)
    xp = jnp.concatenate([prev_sc[...], xbc], axis=0)
    xc = _causal_conv(xp, cw_ref, cb_ref, L)
    convn_ref[...] = _last_rows(xp, t_valid)
    prev_sc[...] = xp[L:L + ROW_PAD]
    xc = xc * _sigmoid(xc)
    xs = xc[:, :SSD_INNER]
    bm = xc[:, SSD_INNER:SSD_INNER + SSD_GROUPS * N].astype(BF16)
    cm = xc[:, SSD_INNER + SSD_GROUPS * N:].astype(BF16)

    row = lax.broadcasted_iota(jnp.int32, (L, LANES), 0)
    lane = lax.broadcasted_iota(jnp.int32, (L, LANES), 1)
    dtr = _pad_rows(dt_ref[...], L) + dtb_ref[...]
    dt = jnp.maximum(dtr, 0.0) + jnp.log1p(jnp.exp(-jnp.abs(dtr)))
    dt = jnp.where((lane < SSD_HEADS) & (row < t_valid), dt, 0.0)
    cs = dt * a_ref[...]
    s = 1
    while s < L:
        cs = jnp.where(row >= s, cs + pltpu.roll(cs, s, 0), cs)
        s *= 2
    ecs = jnp.exp(cs)
    cs_t = cs.T
    dt_t = dt.T
    xs_t = xs.T
    tril = row >= lane
    first_head = lane < SSD_HEAD_DIM
    first_rows = row < SSD_HEAD_DIM

    y_pairs = []
    for hp in range(SSD_HEADS // 2):
        g = (2 * hp) // (SSD_HEADS // SSD_GROUPS)
        bm_g = bm[:, g * N:(g + 1) * N]
        cm_g = cm[:, g * N:(g + 1) * N]
        cb = lax.dot_general(cm_g, bm_g, (((1,), (1,)), ((), ())), preferred_element_type=F32)
        x2 = xs[:, hp * P2:(hp + 1) * P2]
        x2b = x2.astype(BF16)
        yd, wrow, cdec, eoff = [], [], [], []
        for hh in range(2):
            h = 2 * hp + hh
            seg = cs[:, h:h + 1] - cs_t[h:h + 1, :]
            decay = jnp.exp(jnp.where(tril, seg, -jnp.inf))
            wts = cb * decay * dt_t[h:h + 1, :]
            yd.append(jnp.dot(wts.astype(BF16), x2b, preferred_element_type=F32))
            last = cs_t[h:h + 1, L - 1:L]
            wrow.append(jnp.exp(last - cs_t[h:h + 1, :]) * dt_t[h:h + 1, :])
            cdec.append(jnp.exp(last))
            eoff.append(ecs[:, h:h + 1])
        h_pair = h_sc[hp * P2:(hp + 1) * P2, :]
        y_off = lax.dot_general(cm_g, h_pair.astype(BF16), (((1,), (1,)), ((), ())),
                                preferred_element_type=F32)
        y_off = y_off * jnp.where(first_head, eoff[0], eoff[1])
        y_pairs.append(jnp.where(first_head, yd[0], yd[1]) + y_off + d_ref[:, hp * P2:(hp + 1) * P2] * x2)
        xw = xs_t[hp * P2:(hp + 1) * P2, :] * jnp.where(first_rows, wrow[0], wrow[1])
        st = jnp.dot(xw.astype(BF16), bm_g, preferred_element_type=F32)
        h_sc[hp * P2:(hp + 1) * P2, :] = jnp.where(first_rows, cdec[0], cdec[1]) * h_pair + st

    hn_ref[...] = h_sc[...]
    y = jnp.concatenate(y_pairs, axis=1)[:l_in]
    zz = z_ref[...]
    u = y * (zz * _sigmoid(zz))
    gw = SSD_INNER // SSD_GROUPS
    u = jnp.concatenate([_rms(u[:, g * gw:(g + 1) * gw]) for g in range(SSD_GROUPS)], axis=1)
    y_ref[...] = (u * nw_ref[...]).astype(y_ref.dtype)


def _ssd_call(proj, conv0, h0, cw, cb, dtb, a_row, d_lane, nw, *, l_in, t_valid):
    b, t, _ = proj.shape
    assert t % l_in == 0 and (l_in == SSD_CHUNK or t == l_in)
    c = SSD_CONV_DIM
    hp_rows = SSD_HEADS * SSD_HEAD_DIM
    kern = functools.partial(_ssd_kernel, l_in=l_in, t_valid=t_valid)
    return pl.pallas_call(
        kern,
        grid=(b, t // l_in),
        in_specs=[pl.BlockSpec((None, l_in, SSD_INNER), lambda bi, ci: (bi, ci, C_Z // SSD_INNER)),
                  pl.BlockSpec((None, l_in, c), lambda bi, ci: (bi, ci, C_XBC // c)),
                  pl.BlockSpec((None, l_in, LANES), lambda bi, ci: (bi, ci, C_DT // LANES)),
                  pl.BlockSpec((None, ROW_PAD, c), lambda bi, ci: (bi, 0, 0)),
                  pl.BlockSpec((None, hp_rows, SSD_STATE), lambda bi, ci: (bi, 0, 0)),
                  pl.BlockSpec((CONV_WIDTH, c), lambda bi, ci: (0, 0)),
                  pl.BlockSpec((1, c), lambda bi, ci: (0, 0)),
                  pl.BlockSpec((1, LANES), lambda bi, ci: (0, 0)),
                  pl.BlockSpec((1, LANES), lambda bi, ci: (0, 0)),
                  pl.BlockSpec((1, SSD_INNER), lambda bi, ci: (0, 0)),
                  pl.BlockSpec((1, SSD_INNER), lambda bi, ci: (0, 0))],
        out_specs=[pl.BlockSpec((None, l_in, SSD_INNER), lambda bi, ci: (bi, ci, 0)),
                   pl.BlockSpec((None, ROW_PAD, c), lambda bi, ci: (bi, 0, 0)),
                   pl.BlockSpec((None, hp_rows, SSD_STATE), lambda bi, ci: (bi, 0, 0))],
        out_shape=[jax.ShapeDtypeStruct((b, t, SSD_INNER), BF16),
                   jax.ShapeDtypeStruct((b, ROW_PAD, c), F32),
                   jax.ShapeDtypeStruct((b, hp_rows, SSD_STATE), F32)],
        scratch_shapes=[pltpu.VMEM((ROW_PAD, c), F32), pltpu.VMEM((hp_rows, SSD_STATE), F32)],
        compiler_params=_cparams(("arbitrary", "arbitrary")),
        name="ssd",
    )(proj, proj, proj, conv0, h0, cw, cb, dtb, a_row, d_lane, nw)


def _merge_kernel(ya_ref, yb_ref, yc_ref, g0_ref, g1_ref, g2_ref, x_ref, gt_ref, sc_ref, sh_ref,
                  wa_ref, wb_ref, wc_ref, wo_ref, nf_ref, wr_ref, br_ref, x1_ref, h2_ref, lg_ref):
    def branch(y_ref, w_ref, g_ref):
        return _sigmoid(g_ref[...]) * jnp.dot(y_ref[...], w_ref[...], preferred_element_type=F32)

    m = branch(ya_ref, wa_ref, g0_ref) + branch(yb_ref, wb_ref, g1_ref) + branch(yc_ref, wc_ref, g2_ref)
    o = jnp.dot(m.astype(BF16), wo_ref[...], preferred_element_type=F32)
    x1 = x_ref[...] + gt_ref[...] * o
    x1_ref[...] = x1
    h2 = _rms(x1) * nf_ref[...] * (1.0 + sc_ref[...]) + sh_ref[...]
    hi = h2.astype(BF16)
    h2_ref[...] = hi
    lo = (h2 - hi.astype(F32)).astype(BF16)
    lg_ref[...] = (jnp.dot(hi, wr_ref[0], preferred_element_type=F32)
                   + jnp.dot(lo, wr_ref[0], preferred_element_type=F32)
                   + jnp.dot(hi, wr_ref[1], preferred_element_type=F32) + br_ref[...])


def _merge_call(ya, yb, yc, proj, x, gt, sc, sh, wa, wb, wc, wo, nf, wr, br, *, per_row, rows_per_group, tm):
    r = x.shape[0]
    assert r % tm == 0 and (per_row or rows_per_group % tm == 0)
    d = D_MODEL
    ms = _mod_spec(per_row, tm, rows_per_group)
    yspec = pl.BlockSpec((tm, SB_WIDTH), lambda i: (i, 0))
    gspec = [pl.BlockSpec((tm, d), lambda i, k=k: (i, C_BRG // d + k)) for k in range(N_BRANCH)]
    wspec = pl.BlockSpec((SB_WIDTH, d), lambda i: (0, 0))
    return pl.pallas_call(
        _merge_kernel,
        grid=(r // tm,),
        in_specs=[yspec, yspec, yspec, *gspec, pl.BlockSpec((tm, d), lambda i: (i, 0)), ms, ms, ms,
                  wspec, wspec, wspec, pl.BlockSpec((d, d), lambda i: (0, 0)),
                  pl.BlockSpec((1, d), lambda i: (0, 0)),
                  pl.BlockSpec((2, d, LANES), lambda i: (0, 0, 0)), pl.BlockSpec((1, LANES), lambda i: (0, 0))],
        out_specs=[pl.BlockSpec((tm, d), lambda i: (i, 0)), pl.BlockSpec((tm, d), lambda i: (i, 0)),
                   pl.BlockSpec((tm, LANES), lambda i: (i, 0))],
        out_shape=[jax.ShapeDtypeStruct((r, d), F32), jax.ShapeDtypeStruct((r, d), BF16),
                   jax.ShapeDtypeStruct((r, LANES), F32)],
        compiler_params=_cparams(("arbitrary",)),
        name="merge",
    )(ya, yb, yc, proj, proj, proj, x, gt, sc, sh, wa, wb, wc, wo, nf, wr, br)


def _moe_kernel(be_ref, xs_ref, *rest):
    ncb = D_MODEL // LANES
    wu_refs = rest[:ncb]
    bg_ref, bl_ref, wd_ref, bd_ref, o_ref, wg_sc, wl_sc, wd_sc = rest[ncb:]
    i = pl.program_id(0)

    @pl.when((i == 0) | (be_ref[i] != be_ref[jnp.maximum(i - 1, 0)]))
    def _():
        rc = MOE_BM
        for c in range(D_FF // rc):
            rows = slice(c * rc, (c + 1) * rc)
            for cb in range(ncb):
                cols = slice(cb * LANES, (cb + 1) * LANES)
                wg_sc[rows, cols] = wu_refs[cb][pl.ds(2 * c * rc, rc, stride=2), :].astype(BF16)
                wl_sc[rows, cols] = wu_refs[cb][pl.ds(2 * c * rc + 1, rc, stride=2), :].astype(BF16)
            wd_sc[rows, :] = wd_ref[rows, :].astype(BF16)

    x = xs_ref[...]
    nt = (((1,), (1,)), ((), ()))
    glu = lax.dot_general(x, wg_sc[...], nt, preferred_element_type=F32) + bg_ref[...]
    lin = lax.dot_general(x, wl_sc[...], nt, preferred_element_type=F32) + bl_ref[...]
    glu = jnp.minimum(glu, SWIGLU_LIMIT)
    lin = jnp.clip(lin, -SWIGLU_LIMIT, SWIGLU_LIMIT)
    act = glu * _sigmoid(SWIGLU_ALPHA * glu) * (lin + 1.0)
    o_ref[...] = jnp.dot(act.astype(BF16), wd_sc[...], preferred_element_type=F32) + bd_ref[...]


def _moe_call(layer, block_exp, xs, wu_t, bg, bl, wd, bd):
    r = xs.shape[0]
    d = D_MODEL
    assert D_FF % MOE_BM == 0
    bspec = pl.BlockSpec((None, None, 1, d), lambda i, be: (layer, be[i], 0, 0))
    ncb = d // LANES
    wu_specs = [pl.BlockSpec((None, None, 2 * D_FF, LANES), lambda i, be, cb=cb: (layer, be[i], 0, cb))
                for cb in range(ncb)]
    grid_spec = pltpu.PrefetchScalarGridSpec(
        num_scalar_prefetch=1,
        grid=(r // MOE_BM,),
        in_specs=[pl.BlockSpec((MOE_BM, d), lambda i, be: (i, 0)), *wu_specs, bspec, bspec,
                  pl.BlockSpec((None, None, D_FF, d), lambda i, be: (layer, be[i], 0, 0)), bspec],
        out_specs=pl.BlockSpec((MOE_BM, d), lambda i, be: (i, 0)),
        scratch_shapes=[pltpu.VMEM((D_FF, d), BF16), pltpu.VMEM((D_FF, d), BF16), pltpu.VMEM((D_FF, d), BF16)],
    )
    return pl.pallas_call(
        _moe_kernel,
        grid_spec=grid_spec,
        out_shape=jax.ShapeDtypeStruct((r, d), F32),
        compiler_params=_cparams(("arbitrary",), MOE_VMEM_LIMIT),
        name="moe",
    )(block_exp, xs, *([wu_t] * ncb), bg, bl, wd, bd)


def _route(logits, bm):
    n_tok = logits.shape[0]
    n_slot = n_tok * TOP_K
    n_blocks = -(-(n_slot + N_EXPERTS * (bm - 1)) // bm)
    n_rows = n_blocks * bm
    top_val, top_idx = lax.top_k(logits, TOP_K)
    gate = jax.nn.softmax(top_val, axis=-1)
    e_flat = top_idx.reshape(n_slot).astype(jnp.int32)
    order = jnp.argsort(e_flat, stable=True).astype(jnp.int32)
    rank = jnp.argsort(order).astype(jnp.int32)
    experts = jnp.arange(N_EXPERTS, dtype=jnp.int32)
    e_lanes = jnp.pad(e_flat, (0, (-n_slot) % LANES), constant_values=-1).reshape(-1, LANES)
    counts = jnp.sum((e_lanes[None] == experts[:, None, None]).astype(jnp.int32), axis=(1, 2))
    padded = (counts + bm - 1) // bm * bm
    start = jnp.cumsum(counts) - counts
    pend = jnp.cumsum(padded)
    shift = (pend - padded) - start
    block_row = jnp.arange(n_blocks, dtype=jnp.int32) * bm
    block_exp = jnp.minimum(jnp.sum((pend[None, :] <= block_row[:, None]).astype(jnp.int32), axis=1),
                            N_EXPERTS - 1)
    row = block_row[:, None] + jnp.arange(bm, dtype=jnp.int32)[None, :]
    valid = (row - (pend - padded)[block_exp][:, None]) < counts[block_exp][:, None]
    row_slot = order[jnp.clip(row - shift[block_exp][:, None], 0, n_slot - 1).reshape(n_rows)]
    row_tok = jnp.where(valid.reshape(n_rows), row_slot // TOP_K, 0)
    slot_row = (rank + shift[e_flat]).reshape(n_tok, TOP_K)
    return block_exp, row_tok, gate, slot_row.T.reshape(n_slot)


def _resid_kernel(x_ref, y_ref, gate_ref, gt_ref, nw_ref, o_ref, *, final):
    gate = gate_ref[...]
    f = gate[:, 0:1] * y_ref[0]
    for k in range(1, TOP_K):
        f = f + gate[:, k:k + 1] * y_ref[k]
    x = x_ref[...] + gt_ref[...] * f
    if final:
        x = _rms(x) * nw_ref[...]
    o_ref[...] = x


def _resid_call(x, ysel, gate, gt, nw, *, row_off, final, per_row, rows_per_group, tm):
    r = x.shape[0]
    d = D_MODEL
    assert r % tm == 0 and row_off % tm == 0
    off = row_off // tm
    row = pl.BlockSpec((tm, d), lambda i: (i, 0))
    return pl.pallas_call(
        functools.partial(_resid_kernel, final=final),
        grid=(r // tm,),
        in_specs=[row, pl.BlockSpec((TOP_K, tm, d), lambda i: (0, i + off, 0)),
                  pl.BlockSpec((tm, TOP_K), lambda i: (i + off, 0)),
                  _mod_spec(per_row, tm, rows_per_group), pl.BlockSpec((1, d), lambda i: (0, 0))],
        out_specs=row,
        out_shape=jax.ShapeDtypeStruct((r, d), F32),
        compiler_params=_cparams(("arbitrary",)),
        name="resid",
    )(x, ysel, gate, gt, nw)


def _block_diag(w):
    depth, nb, k, _ = w.shape
    eye = jnp.eye(nb, dtype=w.dtype)
    return jnp.einsum('lnij,nm->lnimj', w, eye).reshape(depth, nb * k, nb * k)


def _pick_tile(n, pref):
    t = min(pref, n)
    while n % t:
        t //= 2
    return t


def kernel(x_prompt, x_sample, cache_k, cache_v, state_rg_conv, state_rg_h, state_ssd_conv, state_ssd, page_table, c_prompt, c_sample, w_mod, b_mod, norm_mix, norm_ffn, norm_final, w_in, sb_bias, rg_conv_w, rg_conv_b, rg_w_a, rg_b_a, rg_w_x, rg_b_x, rg_lam, ssd_conv_w, ssd_conv_b, ssd_dt_bias, ssd_a_log, ssd_d, ssd_norm, w_branch, w_out, w_router, b_router, w_up, b_up, w_down, b_down):
    depth = w_in.shape[0]
    b, s, d = x_prompt.shape
    bs, ts, _ = x_sample.shape
    tp = ROW_PAD
    n_p, n_s = b * s, bs * tp
    past_len = page_table.shape[1] * cache_k.shape[2]

    w_in_p = jnp.concatenate(
        [w_in[:, :, :4096], w_in[:, :, 4104:], w_in[:, :, 4096:4104],
         jnp.zeros((depth, d, NP_COLS - w_in.shape[2]), w_in.dtype)], axis=2).astype(BF16)
    wb = w_branch.astype(BF16)
    wb_a, wb_b, wb_c = wb[:, :SB_WIDTH], wb[:, SB_WIDTH:SB_WIDTH + RG_WIDTH], wb[:, SB_WIDTH + RG_WIDTH:]
    wo = w_out.astype(BF16)
    wr_f = jnp.pad(w_router, ((0, 0), (0, 0), (0, LANES - N_EXPERTS)))
    wr_hi = wr_f.astype(BF16)
    wr = jnp.stack([wr_hi, (wr_f - wr_hi.astype(F32)).astype(BF16)], axis=1)
    br = jnp.pad(b_router, ((0, 0), (0, LANES - N_EXPERTS))).reshape(depth, 1, LANES)
    wu_t = jnp.swapaxes(w_up, 2, 3)
    bg = b_up[..., 0::2].reshape(depth, N_EXPERTS, 1, D_FF)
    bl = b_up[..., 1::2].reshape(depth, N_EXPERTS, 1, D_FF)
    bd = b_down.reshape(depth, N_EXPERTS, 1, d)
    rg_wa = _block_diag(rg_w_a).astype(BF16)
    rg_wx = _block_diag(rg_w_x).astype(BF16)
    rg_spl = jax.nn.softplus(-rg_lam)
    ssd_a = jnp.pad(-jnp.exp(ssd_a_log), ((0, 0), (0, LANES - SSD_HEADS)))
    ssd_dtb = jnp.pad(ssd_dt_bias, ((0, 0), (0, LANES - SSD_HEADS)))
    ssd_d_lane = jnp.repeat(ssd_d, SSD_HEAD_DIM, axis=1)
    tri = _tri_matrix(KEY_TILE)
    cache_kt = jnp.transpose(cache_k, (0, 1, 3, 4, 2))
    cache_vt = jnp.transpose(cache_v, (0, 1, 3, 4, 2))

    mod = _mod_call(jnp.concatenate([c_prompt, c_sample], axis=0), w_mod, b_mod)

    def row2(v):
        return v.reshape(1, -1)

    def pad_prev(st):
        return jnp.pad(st, ((0, 0), (tp - (CONV_WIDTH - 1), 0), (0, 0)))

    xp = x_prompt.reshape(n_p, d)
    xq = jnp.pad(x_sample, ((0, 0), (0, tp - ts), (0, 0))).reshape(n_s, d)
    tm_in = _pick_tile(s, 1024)
    tm_mg = _pick_tile(s, 256)
    tt_rg = _pick_tile(s, 256)
    tm_rs = _pick_tile(s, 512)
    outs_p, outs_s = [], []
    for l in range(depth):
        mp = mod[l, :b].reshape(b, 1, 6, d)
        mq = jnp.repeat(mod[l, b:], tp, axis=0).reshape(n_s, 6, d)
        sh1p, sc1p, gt1p, sh2p, sc2p, gt2p = (mp[:, :, k] for k in range(6))
        sh1q, sc1q, gt1q, sh2q, sc2q, gt2q = (mq[:, k] for k in range(6))
        bias_l = sb_bias[l]

        proj_p = _inproj_call(xp, sc1p, sh1p, row2(norm_mix[l]), w_in_p[l],
                              per_row=False, rows_per_group=s, tm=tm_in).reshape(b, s, NP_COLS)
        ya_p = _sbp_call(proj_p, bias_l, tri)
        yb_p, rgc_p, rgh_p = _rg_call(
            proj_p, jnp.zeros((b, tp, RG_WIDTH), F32), jnp.zeros((b, 1, RG_WIDTH), F32),
            rg_conv_w[l], row2(rg_conv_b[l]), rg_wa[l], row2(rg_b_a[l]), rg_wx[l], row2(rg_b_x[l]),
            row2(rg_spl[l]), tt=tt_rg, t_valid=tt_rg, from_start=True)
        yc_p, ssc_p, ss_p = _ssd_call(
            proj_p, jnp.zeros((b, tp, SSD_CONV_DIM), F32),
            jnp.zeros((b, SSD_HEADS * SSD_HEAD_DIM, SSD_STATE), F32),
            ssd_conv_w[l], row2(ssd_conv_b[l]), row2(ssd_dtb[l]), row2(ssd_a[l]), row2(ssd_d_lane[l]),
            row2(ssd_norm[l]), l_in=SSD_CHUNK, t_valid=SSD_CHUNK)
        x1p, h2p, lgp = _merge_call(
            ya_p.reshape(n_p, -1), yb_p.reshape(n_p, -1), yc_p.reshape(n_p, -1), proj_p.reshape(n_p, NP_COLS),
            xp, gt1p, sc2p, sh2p, wb_a[l], wb_b[l], wb_c[l], wo[l], row2(norm_ffn[l]), wr[l], br[l],
            per_row=False, rows_per_group=s, tm=tm_mg)
        outs_p.append((proj_p[:, :, C_K:C_K + SB_WIDTH], proj_p[:, :, C_V:C_V + SB_WIDTH],
                       rgc_p[:, tp - 3:], rgh_p[:, 0], ssc_p[:, tp - 3:], ss_p))

        proj_q = _inproj_call(xq, sc1q, sh1q, row2(norm_mix[l]), w_in_p[l],
                              per_row=True, rows_per_group=tp, tm=n_s).reshape(bs, tp, NP_COLS)
        q_new = proj_q[:, :ts, C_Q:C_Q + SB_WIDTH] * (SB_HEAD_DIM ** -0.5)
        k_new = proj_q[:, :ts, C_K:C_K + SB_WIDTH]
        v_new = proj_q[:, :ts, C_V:C_V + SB_WIDTH]
        head_of_col = jnp.arange(SB_WIDTH) // SB_HEAD_DIM
        own = head_of_col[None, :] == jnp.arange(SB_HEADS)[:, None]
        qm = jnp.where(own[None, None], q_new[:, :, None, :], 0.0).reshape(bs, ts * SB_HEADS, SB_WIDTH)
        bias_rows = jnp.broadcast_to(jnp.tile(bias_l, ts)[:, None], (ts * SB_HEADS, KEY_TILE))
        kn = jnp.pad(jnp.swapaxes(k_new, 1, 2), ((0, 0), (0, 0), (0, KEY_TILE - ts))).astype(BF16)
        vn = jnp.pad(jnp.swapaxes(v_new, 1, 2), ((0, 0), (0, 0), (0, KEY_TILE - ts))).astype(BF16)
        ya_q = _sbs_call(l, page_table, qm.astype(BF16), bias_rows, kn, vn, tri, cache_kt, cache_vt, ts)
        ya_q = jnp.pad(ya_q, ((0, 0), (0, tp - ts), (0, 0))).astype(BF16)
        yb_q, rgc_q, rgh_q = _rg_call(
            proj_q, pad_prev(state_rg_conv[l]), state_rg_h[l][:, None, :],
            rg_conv_w[l], row2(rg_conv_b[l]), rg_wa[l], row2(rg_b_a[l]), rg_wx[l], row2(rg_b_x[l]),
            row2(rg_spl[l]), tt=tp, t_valid=ts, from_start=(past_len == 0))
        yc_q, ssc_q, ss_q = _ssd_call(
            proj_q, pad_prev(state_ssd_conv[l]),
            state_ssd[l].reshape(bs, SSD_HEADS * SSD_HEAD_DIM, SSD_STATE),
            ssd_conv_w[l], row2(ssd_conv_b[l]), row2(ssd_dtb[l]), row2(ssd_a[l]), row2(ssd_d_lane[l]),
            row2(ssd_norm[l]), l_in=tp, t_valid=ts)
        x1q, h2q, lgq = _merge_call(
            ya_q.reshape(n_s, -1), yb_q.reshape(n_s, -1), yc_q.reshape(n_s, -1), proj_q.reshape(n_s, NP_COLS),
            xq, gt1q, sc2q, sh2q, wb_a[l], wb_b[l], wb_c[l], wo[l], row2(norm_ffn[l]), wr[l], br[l],
            per_row=True, rows_per_group=tp, tm=n_s)
        outs_s.append((k_new, v_new, rgc_q[:, tp - 3:], rgh_q[:, 0], ssc_q[:, tp - 3:], ss_q))

        h2 = jnp.concatenate([h2p, h2q], axis=0)
        logits = jnp.concatenate([lgp, lgq], axis=0)[:, :N_EXPERTS]
        block_exp, row_tok, gate, slot_row = _route(logits, MOE_BM)
        ys = _moe_call(l, block_exp, h2[row_tok], wu_t, bg, bl, w_down, bd)
        ysel = ys[slot_row].reshape(TOP_K, n_p + n_s, d)
        final = l == depth - 1
        xp = _resid_call(x1p, ysel, gate, gt2p, row2(norm_final), row_off=0, final=final,
                         per_row=False, rows_per_group=s, tm=tm_rs)
        xq = _resid_call(x1q, ysel, gate, gt2q, row2(norm_final), row_off=n_p, final=final,
                         per_row=True, rows_per_group=tp, tm=n_s)

    def stack(outs, i, shape):
        return jnp.stack([o[i] for o in outs]).reshape(shape)

    y_prompt = xp.reshape(b, s, d)
    y_sample = xq.reshape(bs, tp, d)[:, :ts]
    hs = (SSD_HEADS, SSD_HEAD_DIM, SSD_STATE)
    return (y_prompt, y_sample,
            stack(outs_p, 0, (depth, b, s, SB_HEADS, SB_HEAD_DIM)),
            stack(outs_p, 1, (depth, b, s, SB_HEADS, SB_HEAD_DIM)),
            stack(outs_p, 2, (depth, b, CONV_WIDTH - 1, RG_WIDTH)), stack(outs_p, 3, (depth, b, RG_WIDTH)),
            stack(outs_p, 4, (depth, b, CONV_WIDTH - 1, SSD_CONV_DIM)), stack(outs_p, 5, (depth, b) + hs),
            stack(outs_s, 0, (depth, bs, ts, SB_HEADS, SB_HEAD_DIM)),
            stack(outs_s, 1, (depth, bs, ts, SB_HEADS, SB_HEAD_DIM)),
            stack(outs_s, 2, (depth, bs, CONV_WIDTH - 1, RG_WIDTH)), stack(outs_s, 3, (depth, bs, RG_WIDTH)),
            stack(outs_s, 4, (depth, bs, CONV_WIDTH - 1, SSD_CONV_DIM)), stack(outs_s, 5, (depth, bs) + hs))
```

```python
import functools
import math

import jax
import jax.numpy as jnp
from jax import lax
from jax.experimental import pallas as pl
from jax.experimental.pallas import tpu as pltpu

F32 = jnp.float32
BF16 = jnp.bfloat16
HIGHEST = lax.Precision.HIGHEST

D_MODEL = 1024
SB_HEADS = 8
SB_HEAD_DIM = 64
SB_WIDTH = SB_HEADS * SB_HEAD_DIM
RG_WIDTH = 512
RG_BLOCKS = 8
RG_C = 8.0
CONV_WIDTH = 4
SSD_INNER = 512
SSD_HEAD_DIM = 64
SSD_HEADS = SSD_INNER // SSD_HEAD_DIM
SSD_GROUPS = 2
SSD_STATE = 128
SSD_CHUNK = 128
SSD_CONV_DIM = SSD_INNER + 2 * SSD_GROUPS * SSD_STATE
N_BRANCH = 3
N_EXPERTS = 32
TOP_K = 4
D_FF = D_MODEL
SWIGLU_LIMIT = 7.0
SWIGLU_ALPHA = 1.702
EPS = 1e-6

SUBLANES = 8
LANES = 128
VMEM_LIMIT = 48 * 1024 * 1024

C_Q, C_K, C_V, C_RGX, C_RGG, C_Z, C_XBC, C_BRG, C_DT = 0, 512, 1024, 1536, 2048, 2560, 3072, 4096, 7168
NP_COLS = 7680
IN_TN = 1536
ROW_PAD = 8
KEY_TILE = 128
MOE_BM = 256
PAGES_PER_STEP = 16
SBP_SUBTILES = 8
MOE_VMEM_LIMIT = 56 * 1024 * 1024


def _cparams(sem, vmem=VMEM_LIMIT):
    return pltpu.CompilerParams(dimension_semantics=sem, vmem_limit_bytes=vmem)


def _sigmoid(x):
    return 1.0 / (1.0 + jnp.exp(-x))


def _rms(x):
    return x * lax.rsqrt(jnp.mean(x * x, axis=-1, keepdims=True) + EPS)


def _mod_kernel(c_ref, w_ref, b_ref, o_ref):
    c = c_ref[...]
    s = c * _sigmoid(c)
    o_ref[...] = jnp.dot(s, w_ref[...], precision=HIGHEST, preferred_element_type=F32) + b_ref[...]


def _mod_call(c_all, w_mod, b_mod):
    depth, d, n = w_mod.shape
    nc = c_all.shape[0]
    tn = 1536
    return pl.pallas_call(
        _mod_kernel,
        grid=(depth, n // tn),
        in_specs=[pl.BlockSpec((nc, d), lambda l, j: (0, 0)),
                  pl.BlockSpec((None, d, tn), lambda l, j: (l, 0, j)),
                  pl.BlockSpec((None, 1, tn), lambda l, j: (l, 0, j))],
        out_specs=pl.BlockSpec((None, nc, tn), lambda l, j: (l, 0, j)),
        out_shape=jax.ShapeDtypeStruct((depth, nc, n), F32),
        compiler_params=_cparams(("arbitrary", "arbitrary")),
        name="mod",
    )(c_all, w_mod, b_mod.reshape(depth, 1, n))


def _inproj_kernel(x_ref, sc_ref, sh_ref, g_ref, w_ref, o_ref, h_ref):
    @pl.when(pl.program_id(1) == 0)
    def _():
        h = _rms(x_ref[...]) * g_ref[...]
        h_ref[...] = (h * (1.0 + sc_ref[...]) + sh_ref[...]).astype(BF16)

    o_ref[...] = jnp.dot(h_ref[...], w_ref[...], preferred_element_type=F32)


def _mod_spec(per_row, tm, rows_per_group):
    if per_row:
        return pl.BlockSpec((tm, D_MODEL), lambda i, *_: (i, 0))
    return pl.BlockSpec((None, 1, D_MODEL), lambda i, *_: ((i * tm) // rows_per_group, 0, 0))


def _inproj_call(x, sc, sh, g, w, *, per_row, rows_per_group, tm):
    r = x.shape[0]
    assert r % tm == 0 and (per_row or rows_per_group % tm == 0)
    ms = _mod_spec(per_row, tm, rows_per_group)
    return pl.pallas_call(
        _inproj_kernel,
        grid=(r // tm, NP_COLS // IN_TN),
        in_specs=[pl.BlockSpec((tm, D_MODEL), lambda i, j: (i, 0)), ms, ms,
                  pl.BlockSpec((1, D_MODEL), lambda i, j: (0, 0)),
                  pl.BlockSpec((D_MODEL, IN_TN), lambda i, j: (0, j))],
        out_specs=pl.BlockSpec((tm, IN_TN), lambda i, j: (i, j)),
        out_shape=jax.ShapeDtypeStruct((r, NP_COLS), F32),
        scratch_shapes=[pltpu.VMEM((tm, D_MODEL), BF16)],
        compiler_params=_cparams(("arbitrary", "arbitrary")),
        name="inproj",
    )(x, sc, sh, g, w)


def _sb_logs(z, mask=None):
    sp = jnp.log(1.0 + jnp.exp(-jnp.abs(z)))
    m = jnp.minimum(z, 0.0)
    lk = (m - z) - sp
    if mask is not None:
        lk = jnp.where(mask, lk, 0.0)
    return m - sp, lk


def _sb_mass(lk, tri):
    return jnp.dot(lk.astype(BF16), tri, preferred_element_type=F32)


def _sb_tile(z, tri, c_old, mask):
    t = z.shape[1]
    ls, lk = _sb_logs(z, mask)
    r = _sb_mass(lk, tri)
    w = jnp.exp(ls + r[:, :t] + c_old)
    if mask is not None:
        w = jnp.where(mask, w, 0.0)
    return w, c_old + r[:, t:]


def _tri_matrix(t):
    j = lax.broadcasted_iota(jnp.int32, (t, 2 * t), 0)
    s = lax.broadcasted_iota(jnp.int32, (t, 2 * t), 1)
    return jnp.where((j > s) | (s >= t), 1.0, 0.0).astype(BF16)


def _sbp_kernel(bias_ref, q_ref, k_ref, v_ref, tri_ref, o_ref, kb, vb, qm, acc, cc, *, scale, nq):
    t = KEY_TILE
    hp = pl.program_id(1)
    i = pl.program_id(2)

    @pl.when(i == 0)
    def _():
        kb[...] = k_ref[...].astype(BF16)
        vb[...] = v_ref[...].astype(BF16)

    lane = lax.broadcasted_iota(jnp.int32, (nq * t, LANES), 1)
    first_head = lane < SB_HEAD_DIM
    q = q_ref[...] * scale
    qm[0] = jnp.where(first_head, q, 0.0).astype(BF16)
    qm[1] = jnp.where(first_head, 0.0, q).astype(BF16)
    cc[...] = jnp.zeros_like(cc)
    acc[...] = jnp.zeros_like(acc)
    tri = tri_ref[...]
    row = lax.broadcasted_iota(jnp.int32, (t, t), 0)
    col = lax.broadcasted_iota(jnp.int32, (t, t), 1)
    causal = col < row
    bias = (bias_ref[2 * hp], bias_ref[2 * hp + 1])

    def tile(j, subs):
        start = pl.multiple_of(j * t, t)
        kt = kb[pl.ds(start, t), :]
        vt = vb[pl.ds(start, t), :]
        chains = [(hh, slice(r * t, (r + 1) * t), causal if masked else None)
                  for r, masked in subs for hh in range(2)]
        zs = [lax.dot_general(qm[hh, rows], kt, (((1,), (1,)), ((), ())), preferred_element_type=F32) + bias[hh]
              for hh, rows, _ in chains]
        logs = [_sb_logs(z, mask) for z, (_, _, mask) in zip(zs, chains)]
        mass = [_sb_mass(lk, tri) for _, lk in logs]
        pvs = []
        for (hh, rows, mask), (ls, _), r in zip(chains, logs, mass):
            c_old = cc[hh, rows]
            w = jnp.exp(ls + r[:, :t] + c_old)
            if mask is not None:
                w = jnp.where(mask, w, 0.0)
            cc[hh, rows] = c_old + r[:, t:]
            pvs.append(jnp.dot(w.astype(BF16), vt, preferred_element_type=F32))
        for (hh, rows, _), pv in zip(chains, pvs):
            acc[hh, rows] += pv

    for c in range(nq - 1, -1, -1):
        tile(i * nq + c, [(r, r == c) for r in range(c, nq)])

    def body(jj, carry):
        tile(i * nq - 1 - jj, [(r, False) for r in range(nq)])
        return carry

    lax.fori_loop(0, i * nq, body, 0)
    o_ref[...] = jnp.where(first_head, acc[0], acc[1]).astype(o_ref.dtype)


def _sbp_call(proj, bias, tri):
    b, s, _ = proj.shape
    t = KEY_TILE
    nq = _pick_tile(s // t, SBP_SUBTILES)
    tq = nq * t
    assert s % tq == 0
    kern = functools.partial(_sbp_kernel, scale=SB_HEAD_DIM ** -0.5, nq=nq)
    return pl.pallas_call(
        kern,
        grid=(b, SB_WIDTH // LANES, s // tq),
        in_specs=[pl.BlockSpec(memory_space=pltpu.SMEM),
                  pl.BlockSpec((None, tq, LANES), lambda bi, hp, i: (bi, i, C_Q // LANES + hp)),
                  pl.BlockSpec((None, s, LANES), lambda bi, hp, i: (bi, 0, C_K // LANES + hp)),
                  pl.BlockSpec((None, s, LANES), lambda bi, hp, i: (bi, 0, C_V // LANES + hp)),
                  pl.BlockSpec((t, 2 * t), lambda bi, hp, i: (0, 0))],
        out_specs=pl.BlockSpec((None, tq, LANES), lambda bi, hp, i: (bi, i, hp)),
        out_shape=jax.ShapeDtypeStruct((b, s, SB_WIDTH), BF16),
        scratch_shapes=[pltpu.VMEM((s, LANES), BF16), pltpu.VMEM((s, LANES), BF16),
                        pltpu.VMEM((2, tq, LANES), BF16),
                        pltpu.VMEM((2, tq, LANES), F32), pltpu.VMEM((2, tq, LANES), F32)],
        compiler_params=_cparams(("arbitrary", "arbitrary", "arbitrary")),
        name="sb_prompt",
    )(bias, proj, proj, proj, tri)


def _sbs_kernel(pt_ref, qm_ref, bias_ref, kn_ref, vn_ref, tri_ref, *rest, n_pg, n_q):
    del pt_ref
    k_refs = rest[:n_pg]
    v_refs = rest[n_pg:2 * n_pg]
    o_ref, acc, cc = rest[2 * n_pg:]
    t = KEY_TILE
    m = n_q * SB_HEADS
    p = pl.program_id(1)
    qm = qm_ref[...]
    bias = bias_ref[...]
    tri = tri_ref[...]

    def pv(w, vt):
        return lax.dot_general(w.astype(BF16), vt, (((1,), (1,)), ((), ())), preferred_element_type=F32)

    @pl.when(p == 0)
    def _():
        row = lax.broadcasted_iota(jnp.int32, (m, t), 0)
        col = lax.broadcasted_iota(jnp.int32, (m, t), 1)
        z = jnp.dot(qm, kn_ref[...], preferred_element_type=F32) + bias
        w, c_new = _sb_tile(z, tri, jnp.zeros((m, t), F32), col < row // SB_HEADS)
        cc[...] = c_new
        acc[...] = pv(w, vn_ref[...])

    logs = [_sb_logs(jnp.dot(qm, k_refs[r][...].reshape(SB_WIDTH, t).astype(BF16),
                             preferred_element_type=F32) + bias) for r in range(n_pg)]
    mass = [_sb_mass(lk, tri) for _, lk in logs]
    c = cc[...]
    out = acc[...]
    for r in range(n_pg):
        w = jnp.exp(logs[r][0] + mass[r][:, :t] + c)
        c = c + mass[r][:, t:]
        out = out + pv(w, v_refs[r][...].reshape(SB_WIDTH, t).astype(BF16))
    cc[...] = c
    acc[...] = out

    @pl.when(p == pl.num_programs(1) - 1)
    def _():
        a = acc[...]
        row = lax.broadcasted_iota(jnp.int32, (m, SB_WIDTH), 0)
        lane = lax.broadcasted_iota(jnp.int32, (m, SB_WIDTH), 1)
        own = (lane // SB_HEAD_DIM) == (row % SB_HEADS)
        o_ref[...] = jnp.sum(jnp.where(own, a, 0.0).reshape(n_q, SB_HEADS, SB_WIDTH), axis=1)


def _sbs_call(layer, page_table, qm, bias_rows, kn, vn, tri, cache_kt, cache_vt, n_q):
    bs, n_pages = page_table.shape
    n_pg = min(PAGES_PER_STEP, n_pages)
    assert n_pages % n_pg == 0
    m = n_q * SB_HEADS
    t = KEY_TILE
    assert cache_kt.shape[-1] == t

    def page_spec(r):
        def imap(b, p, pt):
            return (layer, pt[b, n_pages - 1 - (p * n_pg + r)], 0, 0, 0)
        return pl.BlockSpec((None, None, SB_HEADS, SB_HEAD_DIM, t), imap)

    grid_spec = pltpu.PrefetchScalarGridSpec(
        num_scalar_prefetch=1,
        grid=(bs, n_pages // n_pg),
        in_specs=[pl.BlockSpec((None, m, SB_WIDTH), lambda b, p, pt: (b, 0, 0)),
                  pl.BlockSpec((m, t), lambda b, p, pt: (0, 0)),
                  pl.BlockSpec((None, SB_WIDTH, t), lambda b, p, pt: (b, 0, 0)),
                  pl.BlockSpec((None, SB_WIDTH, t), lambda b, p, pt: (b, 0, 0)),
                  pl.BlockSpec((t, 2 * t), lambda b, p, pt: (0, 0))]
        + [page_spec(r) for r in range(n_pg)] + [page_spec(r) for r in range(n_pg)],
        out_specs=pl.BlockSpec((None, n_q, SB_WIDTH), lambda b, p, pt: (b, 0, 0)),
        scratch_shapes=[pltpu.VMEM((m, SB_WIDTH), F32), pltpu.VMEM((m, t), F32)],
    )
    return pl.pallas_call(
        functools.partial(_sbs_kernel, n_pg=n_pg, n_q=n_q),
        grid_spec=grid_spec,
        out_shape=jax.ShapeDtypeStruct((bs, n_q, SB_WIDTH), F32),
        compiler_params=_cparams(("arbitrary", "arbitrary")),
        name="sb_sample",
    )(page_table, qm, bias_rows, kn, vn, tri, *([cache_kt] * n_pg), *([cache_vt] * n_pg))


def _causal_conv(xp, w_ref, b_ref, n):
    y = b_ref[...] + w_ref[CONV_WIDTH - 1:CONV_WIDTH, :] * xp[ROW_PAD:]
    for j in range(CONV_WIDTH - 1):
        y = y + w_ref[j:j + 1, :] * pltpu.roll(xp, CONV_WIDTH - 1 - j, 0)[ROW_PAD:]
    return y


def _last_rows(xp, t_valid):
    n = xp.shape[0]
    if t_valid % SUBLANES == 0:
        return xp[t_valid:t_valid + ROW_PAD]
    return pltpu.roll(xp, n - t_valid, 0)[:ROW_PAD]


def _neg_expm1(x):
    p = x * (1.0 + x * (1 / 2 + x * (1 / 6 + x * (1 / 24 + x * (1 / 120 + x * (1 / 720 + x * (1 / 5040)))))))
    return jnp.where(x > -0.25, -p, 1.0 - jnp.exp(x))


def _gelu_tanh(x):
    return 0.5 * x * (1.0 + jnp.tanh(math.sqrt(2.0 / math.pi) * (x + 0.044715 * (x * x * x))))


def _rg_kernel(x_ref, g_ref, conv0_ref, h0_ref, cw_ref, cb_ref, wa_ref, ba_ref, wx_ref, bx_ref, spl_ref,
               y_ref, convn_ref, hn_ref, prev_sc, h_sc, hbuf, *, tt, t_valid, from_start):
    ti = pl.program_id(1)

    @pl.when(ti == 0)
    def _():
        prev_sc[...] = conv0_ref[...]
        h_sc[...] = h0_ref[...]

    x = x_ref[...]
    xp = jnp.concatenate([prev_sc[...], x], axis=0)
    xc = _causal_conv(xp, cw_ref, cb_ref, tt)
    convn_ref[...] = _last_rows(xp, t_valid)
    prev_sc[...] = xp[tt:tt + ROW_PAD]

    xb = xc.astype(BF16)
    r = _sigmoid(jnp.dot(xb, wa_ref[...], preferred_element_type=F32) + ba_ref[...])
    ig = _sigmoid(jnp.dot(xb, wx_ref[...], preferred_element_type=F32) + bx_ref[...])
    log_a = -RG_C * r * spl_ref[...]
    a = jnp.exp(log_a)
    mult = jnp.sqrt(_neg_expm1(2.0 * log_a))
    rowi = lax.broadcasted_iota(jnp.int32, (tt, RG_WIDTH), 0)
    if from_start:
        mult = jnp.where(rowi + ti * tt == 0, 1.0, mult)
    u = xc * ig * mult
    s = 1
    while s < tt:
        keep = rowi >= s
        u = jnp.where(keep, u + a * pltpu.roll(u, s, 0), u)
        a = jnp.where(keep, a * pltpu.roll(a, s, 0), a)
        s *= 2
    h = u + a * h_sc[...]
    hbuf[...] = h
    h_last = hbuf[t_valid - 1:t_valid, :]
    h_sc[...] = h_last
    hn_ref[...] = h_last
    y_ref[...] = (_gelu_tanh(g_ref[...]) * h).astype(y_ref.dtype)


def _rg_call(proj, conv0, h0, cw, cb, wa, ba, wx, bx, spl, *, tt, t_valid, from_start):
    b, t, _ = proj.shape
    assert t % tt == 0 and (t_valid == tt or t == tt)
    w = RG_WIDTH
    vec = pl.BlockSpec((1, w), lambda bi, ti: (0, 0))
    mat = pl.BlockSpec((w, w), lambda bi, ti: (0, 0))
    kern = functools.partial(_rg_kernel, tt=tt, t_valid=t_valid, from_start=from_start)
    return pl.pallas_call(
        kern,
        grid=(b, t // tt),
        in_specs=[pl.BlockSpec((None, tt, w), lambda bi, ti: (bi, ti, C_RGX // w)),
                  pl.BlockSpec((None, tt, w), lambda bi, ti: (bi, ti, C_RGG // w)),
                  pl.BlockSpec((None, ROW_PAD, w), lambda bi, ti: (bi, 0, 0)),
                  pl.BlockSpec((None, 1, w), lambda bi, ti: (bi, 0, 0)),
                  pl.BlockSpec((CONV_WIDTH, w), lambda bi, ti: (0, 0)), vec, mat, vec, mat, vec, vec],
        out_specs=[pl.BlockSpec((None, tt, w), lambda bi, ti: (bi, ti, 0)),
                   pl.BlockSpec((None, ROW_PAD, w), lambda bi, ti: (bi, 0, 0)),
                   pl.BlockSpec((None, 1, w), lambda bi, ti: (bi, 0, 0))],
        out_shape=[jax.ShapeDtypeStruct((b, t, w), BF16),
                   jax.ShapeDtypeStruct((b, ROW_PAD, w), F32),
                   jax.ShapeDtypeStruct((b, 1, w), F32)],
        scratch_shapes=[pltpu.VMEM((ROW_PAD, w), F32), pltpu.VMEM((1, w), F32), pltpu.VMEM((tt, w), F32)],
        compiler_params=_cparams(("arbitrary", "arbitrary")),
        name="rg_lru",
    )(proj, proj, conv0, h0, cw, cb, wa, ba, wx, bx, spl)


def _pad_rows(x, n):
    if x.shape[0] == n:
        return x
    return jnp.concatenate([x, jnp.zeros((n - x.shape[0], x.shape[1]), x.dtype)], axis=0)


def _ssd_kernel(z_ref, xbc_ref, dt_ref, conv0_ref, h0_ref, cw_ref, cb_ref, dtb_ref, a_ref, d_ref, nw_ref,
                y_ref, convn_ref, hn_ref, prev_sc, h_sc, *, l_in, t_valid):
    L = SSD_CHUNK
    N = SSD_STATE
    P2 = 2 * SSD_HEAD_DIM
    ci = pl.program_id(1)

    @pl.when(ci == 0)
    def _():
        prev_sc[...] = conv0_ref[...]
        h_sc[...] = h0_ref[...]

    xbc = _pad_rows(xbc_ref[...], L)
    xp = jnp.concatenate([prev_sc[...], xbc], axis=0)
    xc = _causal_conv(xp, cw_ref, cb_ref, L)
    convn_ref[...] = _last_rows(xp, t_valid)
    prev_sc[...] = xp[L:L + ROW_PAD]
    xc = xc * _sigmoid(xc)
    xs = xc[:, :SSD_INNER]
    bm = xc[:, SSD_INNER:SSD_INNER + SSD_GROUPS * N].astype(BF16)
    cm = xc[:, SSD_INNER + SSD_GROUPS * N:].astype(BF16)

    row = lax.broadcasted_iota(jnp.int32, (L, LANES), 0)
    lane = lax.broadcasted_iota(jnp.int32, (L, LANES), 1)
    dtr = _pad_rows(dt_ref[...], L) + dtb_ref[...]
    dt = jnp.maximum(dtr, 0.0) + jnp.log1p(jnp.exp(-jnp.abs(dtr)))
    dt = jnp.where((lane < SSD_HEADS) & (row < t_valid), dt, 0.0)
    cs = dt * a_ref[...]
    s = 1
    while s < L:
        cs = jnp.where(row >= s, cs + pltpu.roll(cs, s, 0), cs)
        s *= 2
    ecs = jnp.exp(cs)
    cs_t = cs.T
    dt_t = dt.T
    xs_t = xs.T
    tril = row >= lane
    first_head = lane < SSD_HEAD_DIM
    first_rows = row < SSD_HEAD_DIM

    y_pairs = []
    for hp in range(SSD_HEADS // 2):
        g = (2 * hp) // (SSD_HEADS // SSD_GROUPS)
        bm_g = bm[:, g * N:(g + 1) * N]
        cm_g = cm[:, g * N:(g + 1) * N]
        cb = lax.dot_general(cm_g, bm_g, (((1,), (1,)), ((), ())), preferred_element_type=F32)
        x2 = xs[:, hp * P2:(hp + 1) * P2]
        x2b = x2.astype(BF16)
        yd, wrow, cdec, eoff = [], [], [], []
        for hh in range(2):
            h = 2 * hp + hh
            seg = cs[:, h:h + 1] - cs_t[h:h + 1, :]
            decay = jnp.exp(jnp.where(tril, seg, -jnp.inf))
            wts = cb * decay * dt_t[h:h + 1, :]
            yd.append(jnp.dot(wts.astype(BF16), x2b, preferred_element_type=F32))
            last = cs_t[h:h + 1, L - 1:L]
            wrow.append(jnp.exp(last - cs_t[h:h + 1, :]) * dt_t[h:h + 1, :])
            cdec.append(jnp.exp(last))
            eoff.append(ecs[:, h:h + 1])
        h_pair = h_sc[hp * P2:(hp + 1) * P2, :]
        y_off = lax.dot_general(cm_g, h_pair.astype(BF16), (((1,), (1,)), ((), ())),
                                preferred_element_type=F32)
        y_off = y_off * jnp.where(first_head, eoff[0], eoff[1])
        y_pairs.append(jnp.where(first_head, yd[0], yd[1]) + y_off + d_ref[:, hp * P2:(hp + 1) * P2] * x2)
        xw = xs_t[hp * P2:(hp + 1) * P2, :] * jnp.where(first_rows, wrow[0], wrow[1])
        st = jnp.dot(xw.astype(BF16), bm_g, preferred_element_type=F32)
        h_sc[hp * P2:(hp + 1) * P2, :] = jnp.where(first_rows, cdec[0], cdec[1]) * h_pair + st

    hn_ref[...] = h_sc[...]
    y = jnp.concatenate(y_pairs, axis=1)[:l_in]
    zz = z_ref[...]
    u = y * (zz * _sigmoid(zz))
    gw = SSD_INNER // SSD_GROUPS
    u = jnp.concatenate([_rms(u[:, g * gw:(g + 1) * gw]) for g in range(SSD_GROUPS)], axis=1)
    y_ref[...] = (u * nw_ref[...]).astype(y_ref.dtype)


def _ssd_call(proj, conv0, h0, cw, cb, dtb, a_row, d_lane, nw, *, l_in, t_valid):
    b, t, _ = proj.shape
    assert t % l_in == 0 and (l_in == SSD_CHUNK or t == l_in)
    c = SSD_CONV_DIM
    hp_rows = SSD_HEADS * SSD_HEAD_DIM
    kern = functools.partial(_ssd_kernel, l_in=l_in, t_valid=t_valid)
    return pl.pallas_call(
        kern,
        grid=(b, t // l_in),
        in_specs=[pl.BlockSpec((None, l_in, SSD_INNER), lambda bi, ci: (bi, ci, C_Z // SSD_INNER)),
                  pl.BlockSpec((None, l_in, c), lambda bi, ci: (bi, ci, C_XBC // c)),
                  pl.BlockSpec((None, l_in, LANES), lambda bi, ci: (bi, ci, C_DT // LANES)),
                  pl.BlockSpec((None, ROW_PAD, c), lambda bi, ci: (bi, 0, 0)),
                  pl.BlockSpec((None, hp_rows, SSD_STATE), lambda bi, ci: (bi, 0, 0)),
                  pl.BlockSpec((CONV_WIDTH, c), lambda bi, ci: (0, 0)),
                  pl.BlockSpec((1, c), lambda bi, ci: (0, 0)),
                  pl.BlockSpec((1, LANES), lambda bi, ci: (0, 0)),
                  pl.BlockSpec((1, LANES), lambda bi, ci: (0, 0)),
                  pl.BlockSpec((1, SSD_INNER), lambda bi, ci: (0, 0)),
                  pl.BlockSpec((1, SSD_INNER), lambda bi, ci: (0, 0))],
        out_specs=[pl.BlockSpec((None, l_in, SSD_INNER), lambda bi, ci: (bi, ci, 0)),
                   pl.BlockSpec((None, ROW_PAD, c), lambda bi, ci: (bi, 0, 0)),
                   pl.BlockSpec((None, hp_rows, SSD_STATE), lambda bi, ci: (bi, 0, 0))],
        out_shape=[jax.ShapeDtypeStruct((b, t, SSD_INNER), BF16),
                   jax.ShapeDtypeStruct((b, ROW_PAD, c), F32),
                   jax.ShapeDtypeStruct((b, hp_rows, SSD_STATE), F32)],
        scratch_shapes=[pltpu.VMEM((ROW_PAD, c), F32), pltpu.VMEM((hp_rows, SSD_STATE), F32)],
        compiler_params=_cparams(("arbitrary", "arbitrary")),
        name="ssd",
    )(proj, proj, proj, conv0, h0, cw, cb, dtb, a_row, d_lane, nw)


def _merge_kernel(ya_ref, yb_ref, yc_ref, g0_ref, g1_ref, g2_ref, x_ref, gt_ref, sc_ref, sh_ref,
                  wa_ref, wb_ref, wc_ref, wo_ref, nf_ref, wr_ref, br_ref, x1_ref, h2_ref, lg_ref):
    def branch(y_ref, w_ref, g_ref):
        return _sigmoid(g_ref[...]) * jnp.dot(y_ref[...], w_ref[...], preferred_element_type=F32)

    m = branch(ya_ref, wa_ref, g0_ref) + branch(yb_ref, wb_ref, g1_ref) + branch(yc_ref, wc_ref, g2_ref)
    o = jnp.dot(m.astype(BF16), wo_ref[...], preferred_element_type=F32)
    x1 = x_ref[...] + gt_ref[...] * o
    x1_ref[...] = x1
    h2 = _rms(x1) * nf_ref[...] * (1.0 + sc_ref[...]) + sh_ref[...]
    hi = h2.astype(BF16)
    h2_ref[...] = hi
    lo = (h2 - hi.astype(F32)).astype(BF16)
    lg_ref[...] = (jnp.dot(hi, wr_ref[0], preferred_element_type=F32)
                   + jnp.dot(lo, wr_ref[0], preferred_element_type=F32)
                   + jnp.dot(hi, wr_ref[1], preferred_element_type=F32) + br_ref[...])


def _merge_call(ya, yb, yc, proj, x, gt, sc, sh, wa, wb, wc, wo, nf, wr, br, *, per_row, rows_per_group, tm):
    r = x.shape[0]
    assert r % tm == 0 and (per_row or rows_per_group % tm == 0)
    d = D_MODEL
    ms = _mod_spec(per_row, tm, rows_per_group)
    yspec = pl.BlockSpec((tm, SB_WIDTH), lambda i: (i, 0))
    gspec = [pl.BlockSpec((tm, d), lambda i, k=k: (i, C_BRG // d + k)) for k in range(N_BRANCH)]
    wspec = pl.BlockSpec((SB_WIDTH, d), lambda i: (0, 0))
    return pl.pallas_call(
        _merge_kernel,
        grid=(r // tm,),
        in_specs=[yspec, yspec, yspec, *gspec, pl.BlockSpec((tm, d), lambda i: (i, 0)), ms, ms, ms,
                  wspec, wspec, wspec, pl.BlockSpec((d, d), lambda i: (0, 0)),
                  pl.BlockSpec((1, d), lambda i: (0, 0)),
                  pl.BlockSpec((2, d, LANES), lambda i: (0, 0, 0)), pl.BlockSpec((1, LANES), lambda i: (0, 0))],
        out_specs=[pl.BlockSpec((tm, d), lambda i: (i, 0)), pl.BlockSpec((tm, d), lambda i: (i, 0)),
                   pl.BlockSpec((tm, LANES), lambda i: (i, 0))],
        out_shape=[jax.ShapeDtypeStruct((r, d), F32), jax.ShapeDtypeStruct((r, d), BF16),
                   jax.ShapeDtypeStruct((r, LANES), F32)],
        compiler_params=_cparams(("arbitrary",)),
        name="merge",
    )(ya, yb, yc, proj, proj, proj, x, gt, sc, sh, wa, wb, wc, wo, nf, wr, br)


def _moe_kernel(be_ref, xs_ref, *rest):
    ncb = D_MODEL // LANES
    wu_refs = rest[:ncb]
    bg_ref, bl_ref, wd_ref, bd_ref, o_ref, wg_sc, wl_sc, wd_sc = rest[ncb:]
    i = pl.program_id(0)

    @pl.when((i == 0) | (be_ref[i] != be_ref[jnp.maximum(i - 1, 0)]))
    def _():
        rc = MOE_BM
        for c in range(D_FF // rc):
            rows = slice(c * rc, (c + 1) * rc)
            for cb in range(ncb):
                cols = slice(cb * LANES, (cb + 1) * LANES)
                wg_sc[rows, cols] = wu_refs[cb][pl.ds(2 * c * rc, rc, stride=2), :].astype(BF16)
                wl_sc[rows, cols] = wu_refs[cb][pl.ds(2 * c * rc + 1, rc, stride=2), :].astype(BF16)
            wd_sc[rows, :] = wd_ref[rows, :].astype(BF16)

    nt = (((1,), (1,)), ((), ()))
    n_active = be_ref[pl.num_programs(0)]

    @pl.when(i < n_active)
    def _():
        x = xs_ref[...]
        glu = lax.dot_general(x, wg_sc[...], nt, preferred_element_type=F32) + bg_ref[...]
        lin = lax.dot_general(x, wl_sc[...], nt, preferred_element_type=F32) + bl_ref[...]
        glu = jnp.minimum(glu, SWIGLU_LIMIT)
        lin = jnp.clip(lin, -SWIGLU_LIMIT, SWIGLU_LIMIT)
        act = glu * _sigmoid(SWIGLU_ALPHA * glu) * (lin + 1.0)
        y = jnp.dot(act.astype(BF16), wd_sc[...], preferred_element_type=F32) + bd_ref[...]
        o_ref[...] = y.astype(o_ref.dtype)

    @pl.when(i >= n_active)
    def _():
        o_ref[...] = jnp.zeros_like(o_ref)


def _moe_call(layer, block_exp, xs, wu_t, bg, bl, wd, bd):
    r = xs.shape[0]
    d = D_MODEL
    assert D_FF % MOE_BM == 0
    bspec = pl.BlockSpec((None, None, 1, d), lambda i, be: (layer, be[i], 0, 0))
    ncb = d // LANES
    wu_specs = [pl.BlockSpec((None, None, 2 * D_FF, LANES), lambda i, be, cb=cb: (layer, be[i], 0, cb))
                for cb in range(ncb)]
    grid_spec = pltpu.PrefetchScalarGridSpec(
        num_scalar_prefetch=1,
        grid=(r // MOE_BM,),
        in_specs=[pl.BlockSpec((MOE_BM, d), lambda i, be: (i, 0)), *wu_specs, bspec, bspec,
                  pl.BlockSpec((None, None, D_FF, d), lambda i, be: (layer, be[i], 0, 0)), bspec],
        out_specs=pl.BlockSpec((MOE_BM, d), lambda i, be: (i, 0)),
        scratch_shapes=[pltpu.VMEM((D_FF, d), BF16), pltpu.VMEM((D_FF, d), BF16), pltpu.VMEM((D_FF, d), BF16)],
    )
    return pl.pallas_call(
        _moe_kernel,
        grid_spec=grid_spec,
        out_shape=jax.ShapeDtypeStruct((r, d), BF16),
        compiler_params=_cparams(("arbitrary",), MOE_VMEM_LIMIT),
        name="moe",
    )(block_exp, xs, *([wu_t] * ncb), bg, bl, wd, bd)


def _route(logits, bm):
    n_tok = logits.shape[0]
    n_slot = n_tok * TOP_K
    n_blocks = -(-(n_slot + N_EXPERTS * (bm - 1)) // bm)
    n_rows = n_blocks * bm
    top_val, top_idx = lax.top_k(logits, TOP_K)
    gate = jax.nn.softmax(top_val, axis=-1)
    e_flat = top_idx.reshape(n_slot).astype(jnp.int32)
    order = jnp.argsort(e_flat, stable=True).astype(jnp.int32)
    rank = jnp.argsort(order).astype(jnp.int32)
    experts = jnp.arange(N_EXPERTS, dtype=jnp.int32)
    e_lanes = jnp.pad(e_flat, (0, (-n_slot) % LANES), constant_values=-1).reshape(-1, LANES)
    counts = jnp.sum((e_lanes[None] == experts[:, None, None]).astype(jnp.int32), axis=(1, 2))
    padded = (counts + bm - 1) // bm * bm
    start = jnp.cumsum(counts) - counts
    pend = jnp.cumsum(padded)
    shift = (pend - padded) - start
    block_row = jnp.arange(n_blocks, dtype=jnp.int32) * bm
    block_exp = jnp.minimum(jnp.sum((pend[None, :] <= block_row[:, None]).astype(jnp.int32), axis=1),
                            N_EXPERTS - 1)
    row = block_row[:, None] + jnp.arange(bm, dtype=jnp.int32)[None, :]
    valid = (row - (pend - padded)[block_exp][:, None]) < counts[block_exp][:, None]
    row_slot = order[jnp.clip(row - shift[block_exp][:, None], 0, n_slot - 1).reshape(n_rows)]
    row_tok = jnp.where(valid.reshape(n_rows), row_slot // TOP_K, 0)
    slot_row = (rank + shift[e_flat]).reshape(n_tok, TOP_K)
    block_plan = jnp.concatenate([block_exp, pend[N_EXPERTS - 1:] // bm]).astype(jnp.int32)
    return block_plan, row_tok, gate, slot_row.T.reshape(n_slot)


def _resid_kernel(x_ref, y_ref, gate_ref, gt_ref, nw_ref, o_ref, *, final):
    gate = gate_ref[...]
    f = gate[:, 0:1] * y_ref[0]
    for k in range(1, TOP_K):
        f = f + gate[:, k:k + 1] * y_ref[k]
    x = x_ref[...] + gt_ref[...] * f
    if final:
        x = _rms(x) * nw_ref[...]
    o_ref[...] = x


def _resid_call(x, ysel, gate, gt, nw, *, row_off, final, per_row, rows_per_group, tm):
    r = x.shape[0]
    d = D_MODEL
    assert r % tm == 0 and row_off % tm == 0
    off = row_off // tm
    row = pl.BlockSpec((tm, d), lambda i: (i, 0))
    return pl.pallas_call(
        functools.partial(_resid_kernel, final=final),
        grid=(r // tm,),
        in_specs=[row, pl.BlockSpec((TOP_K, tm, d), lambda i: (0, i + off, 0)),
                  pl.BlockSpec((tm, TOP_K), lambda i: (i + off, 0)),
                  _mod_spec(per_row, tm, rows_per_group), pl.BlockSpec((1, d), lambda i: (0, 0))],
        out_specs=row,
        out_shape=jax.ShapeDtypeStruct((r, d), F32),
        compiler_params=_cparams(("arbitrary",)),
        name="resid",
    )(x, ysel, gate, gt, nw)


def _block_diag(w):
    depth, nb, k, _ = w.shape
    eye = jnp.eye(nb, dtype=w.dtype)
    return jnp.einsum('lnij,nm->lnimj', w, eye).reshape(depth, nb * k, nb * k)


def _pick_tile(n, pref):
    t = min(pref, n)
    while n % t:
        t //= 2
    return t


def kernel(x_prompt, x_sample, cache_k, cache_v, state_rg_conv, state_rg_h, state_ssd_conv, state_ssd, page_table, c_prompt, c_sample, w_mod, b_mod, norm_mix, norm_ffn, norm_final, w_in, sb_bias, rg_conv_w, rg_conv_b, rg_w_a, rg_b_a, rg_w_x, rg_b_x, rg_lam, ssd_conv_w, ssd_conv_b, ssd_dt_bias, ssd_a_log, ssd_d, ssd_norm, w_branch, w_out, w_router, b_router, w_up, b_up, w_down, b_down):
    depth = w_in.shape[0]
    b, s, d = x_prompt.shape
    bs, ts, _ = x_sample.shape
    tp = ROW_PAD
    n_p, n_s = b * s, bs * tp
    past_len = page_table.shape[1] * cache_k.shape[2]

    w_in_p = jnp.concatenate(
        [w_in[:, :, :4096], w_in[:, :, 4104:], w_in[:, :, 4096:4104],
         jnp.zeros((depth, d, NP_COLS - w_in.shape[2]), w_in.dtype)], axis=2).astype(BF16)
    wb = w_branch.astype(BF16)
    wb_a, wb_b, wb_c = wb[:, :SB_WIDTH], wb[:, SB_WIDTH:SB_WIDTH + RG_WIDTH], wb[:, SB_WIDTH + RG_WIDTH:]
    wo = w_out.astype(BF16)
    wr_f = jnp.pad(w_router, ((0, 0), (0, 0), (0, LANES - N_EXPERTS)))
    wr_hi = wr_f.astype(BF16)
    wr = jnp.stack([wr_hi, (wr_f - wr_hi.astype(F32)).astype(BF16)], axis=1)
    br = jnp.pad(b_router, ((0, 0), (0, LANES - N_EXPERTS))).reshape(depth, 1, LANES)
    wu_t = jnp.swapaxes(w_up, 2, 3)
    bg = b_up[..., 0::2].reshape(depth, N_EXPERTS, 1, D_FF)
    bl = b_up[..., 1::2].reshape(depth, N_EXPERTS, 1, D_FF)
    bd = b_down.reshape(depth, N_EXPERTS, 1, d)
    rg_wa = _block_diag(rg_w_a).astype(BF16)
    rg_wx = _block_diag(rg_w_x).astype(BF16)
    rg_spl = jax.nn.softplus(-rg_lam)
    ssd_a = jnp.pad(-jnp.exp(ssd_a_log), ((0, 0), (0, LANES - SSD_HEADS)))
    ssd_dtb = jnp.pad(ssd_dt_bias, ((0, 0), (0, LANES - SSD_HEADS)))
    ssd_d_lane = jnp.repeat(ssd_d, SSD_HEAD_DIM, axis=1)
    tri = _tri_matrix(KEY_TILE)
    cache_kt = jnp.transpose(cache_k, (0, 1, 3, 4, 2))
    cache_vt = jnp.transpose(cache_v, (0, 1, 3, 4, 2))

    mod = _mod_call(jnp.concatenate([c_prompt, c_sample], axis=0), w_mod, b_mod)

    def row2(v):
        return v.reshape(1, -1)

    def pad_prev(st):
        return jnp.pad(st, ((0, 0), (tp - (CONV_WIDTH - 1), 0), (0, 0)))

    xp = x_prompt.reshape(n_p, d)
    xq = jnp.pad(x_sample, ((0, 0), (0, tp - ts), (0, 0))).reshape(n_s, d)
    tm_in = _pick_tile(s, 1024)
    tm_mg = _pick_tile(s, 256)
    tt_rg = _pick_tile(s, 256)
    tm_rs = _pick_tile(s, 512)
    outs_p, outs_s = [], []
    for l in range(depth):
        mp = mod[l, :b].reshape(b, 1, 6, d)
        mq = jnp.repeat(mod[l, b:], tp, axis=0).reshape(n_s, 6, d)
        sh1p, sc1p, gt1p, sh2p, sc2p, gt2p = (mp[:, :, k] for k in range(6))
        sh1q, sc1q, gt1q, sh2q, sc2q, gt2q = (mq[:, k] for k in range(6))
        bias_l = sb_bias[l]

        proj_p = _inproj_call(xp, sc1p, sh1p, row2(norm_mix[l]), w_in_p[l],
                              per_row=False, rows_per_group=s, tm=tm_in).reshape(b, s, NP_COLS)
        ya_p = _sbp_call(proj_p, bias_l, tri)
        yb_p, rgc_p, rgh_p = _rg_call(
            proj_p, jnp.zeros((b, tp, RG_WIDTH), F32), jnp.zeros((b, 1, RG_WIDTH), F32),
            rg_conv_w[l], row2(rg_conv_b[l]), rg_wa[l], row2(rg_b_a[l]), rg_wx[l], row2(rg_b_x[l]),
            row2(rg_spl[l]), tt=tt_rg, t_valid=tt_rg, from_start=True)
        yc_p, ssc_p, ss_p = _ssd_call(
            proj_p, jnp.zeros((b, tp, SSD_CONV_DIM), F32),
            jnp.zeros((b, SSD_HEADS * SSD_HEAD_DIM, SSD_STATE), F32),
            ssd_conv_w[l], row2(ssd_conv_b[l]), row2(ssd_dtb[l]), row2(ssd_a[l]), row2(ssd_d_lane[l]),
            row2(ssd_norm[l]), l_in=SSD_CHUNK, t_valid=SSD_CHUNK)
        x1p, h2p, lgp = _merge_call(
            ya_p.reshape(n_p, -1), yb_p.reshape(n_p, -1), yc_p.reshape(n_p, -1), proj_p.reshape(n_p, NP_COLS),
            xp, gt1p, sc2p, sh2p, wb_a[l], wb_b[l], wb_c[l], wo[l], row2(norm_ffn[l]), wr[l], br[l],
            per_row=False, rows_per_group=s, tm=tm_mg)
        outs_p.append((proj_p[:, :, C_K:C_K + SB_WIDTH], proj_p[:, :, C_V:C_V + SB_WIDTH],
                       rgc_p[:, tp - 3:], rgh_p[:, 0], ssc_p[:, tp - 3:], ss_p))

        proj_q = _inproj_call(xq, sc1q, sh1q, row2(norm_mix[l]), w_in_p[l],
                              per_row=True, rows_per_group=tp, tm=n_s).reshape(bs, tp, NP_COLS)
        q_new = proj_q[:, :ts, C_Q:C_Q + SB_WIDTH] * (SB_HEAD_DIM ** -0.5)
        k_new = proj_q[:, :ts, C_K:C_K + SB_WIDTH]
        v_new = proj_q[:, :ts, C_V:C_V + SB_WIDTH]
        head_of_col = jnp.arange(SB_WIDTH) // SB_HEAD_DIM
        own = head_of_col[None, :] == jnp.arange(SB_HEADS)[:, None]
        qm = jnp.where(own[None, None], q_new[:, :, None, :], 0.0).reshape(bs, ts * SB_HEADS, SB_WIDTH)
        bias_rows = jnp.broadcast_to(jnp.tile(bias_l, ts)[:, None], (ts * SB_HEADS, KEY_TILE))
        kn = jnp.pad(jnp.swapaxes(k_new, 1, 2), ((0, 0), (0, 0), (0, KEY_TILE - ts))).astype(BF16)
        vn = jnp.pad(jnp.swapaxes(v_new, 1, 2), ((0, 0), (0, 0), (0, KEY_TILE - ts))).astype(BF16)
        ya_q = _sbs_call(l, page_table, qm.astype(BF16), bias_rows, kn, vn, tri, cache_kt, cache_vt, ts)
        ya_q = jnp.pad(ya_q, ((0, 0), (0, tp - ts), (0, 0))).astype(BF16)
        yb_q, rgc_q, rgh_q = _rg_call(
            proj_q, pad_prev(state_rg_conv[l]), state_rg_h[l][:, None, :],
            rg_conv_w[l], row2(rg_conv_b[l]), rg_wa[l], row2(rg_b_a[l]), rg_wx[l], row2(rg_b_x[l]),
            row2(rg_spl[l]), tt=tp, t_valid=ts, from_start=(past_len == 0))
        yc_q, ssc_q, ss_q = _ssd_call(
            proj_q, pad_prev(state_ssd_conv[l]),
            state_ssd[l].reshape(bs, SSD_HEADS * SSD_HEAD_DIM, SSD_STATE),
            ssd_conv_w[l], row2(ssd_conv_b[l]), row2(ssd_dtb[l]), row2(ssd_a[l]), row2(ssd_d_lane[l]),
            row2(ssd_norm[l]), l_in=tp, t_valid=ts)
        x1q, h2q, lgq = _merge_call(
            ya_q.reshape(n_s, -1), yb_q.reshape(n_s, -1), yc_q.reshape(n_s, -1), proj_q.reshape(n_s, NP_COLS),
            xq, gt1q, sc2q, sh2q, wb_a[l], wb_b[l], wb_c[l], wo[l], row2(norm_ffn[l]), wr[l], br[l],
            per_row=True, rows_per_group=tp, tm=n_s)
        outs_s.append((k_new, v_new, rgc_q[:, tp - 3:], rgh_q[:, 0], ssc_q[:, tp - 3:], ss_q))

        h2 = jnp.concatenate([h2p, h2q], axis=0)
        logits = jnp.concatenate([lgp, lgq], axis=0)[:, :N_EXPERTS]
        block_exp, row_tok, gate, slot_row = _route(logits, MOE_BM)
        ys = _moe_call(l, block_exp, h2[row_tok], wu_t, bg, bl, w_down, bd)
        ysel = ys[slot_row].reshape(TOP_K, n_p + n_s, d)
        final = l == depth - 1
        xp = _resid_call(x1p, ysel, gate, gt2p, row2(norm_final), row_off=0, final=final,
                         per_row=False, rows_per_group=s, tm=tm_rs)
        xq = _resid_call(x1q, ysel, gate, gt2q, row2(norm_final), row_off=n_p, final=final,
                         per_row=True, rows_per_group=tp, tm=n_s)

    def stack(outs, i, shape):
        return jnp.stack([o[i] for o in outs]).reshape(shape)

    y_prompt = xp.reshape(b, s, d)
    y_sample = xq.reshape(bs, tp, d)[:, :ts]
    hs = (SSD_HEADS, SSD_HEAD_DIM, SSD_STATE)
    return (y_prompt, y_sample,
            stack(outs_p, 0, (depth, b, s, SB_HEADS, SB_HEAD_DIM)),
            stack(outs_p, 1, (depth, b, s, SB_HEADS, SB_HEAD_DIM)),
            stack(outs_p, 2, (depth, b, CONV_WIDTH - 1, RG_WIDTH)), stack(outs_p, 3, (depth, b, RG_WIDTH)),
            stack(outs_p, 4, (depth, b, CONV_WIDTH - 1, SSD_CONV_DIM)), stack(outs_p, 5, (depth, b) + hs),
            stack(outs_s, 0, (depth, bs, ts, SB_HEADS, SB_HEAD_DIM)),
            stack(outs_s, 1, (depth, bs, ts, SB_HEADS, SB_HEAD_DIM)),
            stack(outs_s, 2, (depth, bs, CONV_WIDTH - 1, RG_WIDTH)), stack(outs_s, 3, (depth, bs, RG_WIDTH)),
            stack(outs_s, 4, (depth, bs, CONV_WIDTH - 1, SSD_CONV_DIM)), stack(outs_s, 5, (depth, bs) + hs))
```

```python
import functools
import math

import jax
import jax.numpy as jnp
from jax import lax
from jax.experimental import pallas as pl
from jax.experimental.pallas import tpu as pltpu

F32 = jnp.float32
BF16 = jnp.bfloat16
HIGHEST = lax.Precision.HIGHEST

D_MODEL = 1024
SB_HEADS = 8
SB_HEAD_DIM = 64
SB_WIDTH = SB_HEADS * SB_HEAD_DIM
RG_WIDTH = 512
RG_BLOCKS = 8
RG_C = 8.0
CONV_WIDTH = 4
SSD_INNER = 512
SSD_HEAD_DIM = 64
SSD_HEADS = SSD_INNER // SSD_HEAD_DIM
SSD_GROUPS = 2
SSD_STATE = 128
SSD_CHUNK = 128
SSD_CONV_DIM = SSD_INNER + 2 * SSD_GROUPS * SSD_STATE
N_BRANCH = 3
N_EXPERTS = 32
TOP_K = 4
D_FF = D_MODEL
SWIGLU_LIMIT = 7.0
SWIGLU_ALPHA = 1.702
EPS = 1e-6

SUBLANES = 8
LANES = 128
VMEM_LIMIT = 48 * 1024 * 1024

C_Q, C_K, C_V, C_RGX, C_RGG, C_Z, C_XBC, C_BRG, C_DT = 0, 512, 1024, 1536, 2048, 2560, 3072, 4096, 7168
NP_COLS = 7680
IN_TN = 1536
ROW_PAD = 8
KEY_TILE = 128
MOE_BM = 256
PAGES_PER_STEP = 16
SBP_SUBTILES = 8
MOE_VMEM_LIMIT = 56 * 1024 * 1024


def _cparams(sem, vmem=VMEM_LIMIT):
    return pltpu.CompilerParams(dimension_semantics=sem, vmem_limit_bytes=vmem)


def _sigmoid(x):
    return 1.0 / (1.0 + jnp.exp(-x))


def _rms(x):
    return x * lax.rsqrt(jnp.mean(x * x, axis=-1, keepdims=True) + EPS)


def _mod_kernel(c_ref, w_ref, b_ref, o_ref):
    c = c_ref[...]
    s = c * _sigmoid(c)
    o_ref[...] = jnp.dot(s, w_ref[...], precision=HIGHEST, preferred_element_type=F32) + b_ref[...]


def _mod_call(c_all, w_mod, b_mod):
    depth, d, n = w_mod.shape
    nc = c_all.shape[0]
    tn = 1536
    return pl.pallas_call(
        _mod_kernel,
        grid=(depth, n // tn),
        in_specs=[pl.BlockSpec((nc, d), lambda l, j: (0, 0)),
                  pl.BlockSpec((None, d, tn), lambda l, j: (l, 0, j)),
                  pl.BlockSpec((None, 1, tn), lambda l, j: (l, 0, j))],
        out_specs=pl.BlockSpec((None, nc, tn), lambda l, j: (l, 0, j)),
        out_shape=jax.ShapeDtypeStruct((depth, nc, n), F32),
        compiler_params=_cparams(("arbitrary", "arbitrary")),
        name="mod",
    )(c_all, w_mod, b_mod.reshape(depth, 1, n))


def _inproj_kernel(x_ref, sc_ref, sh_ref, g_ref, w_ref, o_ref, h_ref):
    @pl.when(pl.program_id(1) == 0)
    def _():
        h = _rms(x_ref[...]) * g_ref[...]
        h_ref[...] = (h * (1.0 + sc_ref[...]) + sh_ref[...]).astype(BF16)

    o_ref[...] = jnp.dot(h_ref[...], w_ref[...], preferred_element_type=F32)


def _mod_spec(per_row, tm, rows_per_group):
    if per_row:
        return pl.BlockSpec((tm, D_MODEL), lambda i, *_: (i, 0))
    return pl.BlockSpec((None, 1, D_MODEL), lambda i, *_: ((i * tm) // rows_per_group, 0, 0))


def _inproj_call(x, sc, sh, g, w, *, per_row, rows_per_group, tm):
    r = x.shape[0]
    assert r % tm == 0 and (per_row or rows_per_group % tm == 0)
    ms = _mod_spec(per_row, tm, rows_per_group)
    return pl.pallas_call(
        _inproj_kernel,
        grid=(r // tm, NP_COLS // IN_TN),
        in_specs=[pl.BlockSpec((tm, D_MODEL), lambda i, j: (i, 0)), ms, ms,
                  pl.BlockSpec((1, D_MODEL), lambda i, j: (0, 0)),
                  pl.BlockSpec((D_MODEL, IN_TN), lambda i, j: (0, j))],
        out_specs=pl.BlockSpec((tm, IN_TN), lambda i, j: (i, j)),
        out_shape=jax.ShapeDtypeStruct((r, NP_COLS), F32),
        scratch_shapes=[pltpu.VMEM((tm, D_MODEL), BF16)],
        compiler_params=_cparams(("arbitrary", "arbitrary")),
        name="inproj",
    )(x, sc, sh, g, w)


def _sb_logs(z, mask=None):
    sp = jnp.log(1.0 + jnp.exp(-jnp.abs(z)))
    m = jnp.minimum(z, 0.0)
    lk = (m - z) - sp
    if mask is not None:
        lk = jnp.where(mask, lk, 0.0)
    return m - sp, lk


def _sb_mass(lk, tri):
    return jnp.dot(lk.astype(BF16), tri, preferred_element_type=F32)


def _sb_tile(z, tri, c_old, mask):
    t = z.shape[1]
    ls, lk = _sb_logs(z, mask)
    r = _sb_mass(lk, tri)
    w = jnp.exp(ls + r[:, :t] + c_old)
    if mask is not None:
        w = jnp.where(mask, w, 0.0)
    return w, c_old + r[:, t:]


def _tri_matrix(t):
    j = lax.broadcasted_iota(jnp.int32, (t, 2 * t), 0)
    s = lax.broadcasted_iota(jnp.int32, (t, 2 * t), 1)
    return jnp.where((j > s) | (s >= t), 1.0, 0.0).astype(BF16)


def _sbp_kernel(bias_ref, q_ref, k_ref, v_ref, tri_ref, o_ref, kb, vb, qm, acc, cc, *, scale, nq):
    t = KEY_TILE
    hp = pl.program_id(1)
    i = pl.program_id(2)

    @pl.when(i == 0)
    def _():
        kb[...] = k_ref[...].astype(BF16)
        vb[...] = v_ref[...].astype(BF16)

    lane = lax.broadcasted_iota(jnp.int32, (nq * t, LANES), 1)
    first_head = lane < SB_HEAD_DIM
    q = q_ref[...] * scale
    qm[0] = jnp.where(first_head, q, 0.0).astype(BF16)
    qm[1] = jnp.where(first_head, 0.0, q).astype(BF16)
    cc[...] = jnp.zeros_like(cc)
    acc[...] = jnp.zeros_like(acc)
    tri = tri_ref[...]
    row = lax.broadcasted_iota(jnp.int32, (t, t), 0)
    col = lax.broadcasted_iota(jnp.int32, (t, t), 1)
    causal = col < row
    bias = (bias_ref[2 * hp], bias_ref[2 * hp + 1])

    def tile(j, subs):
        start = pl.multiple_of(j * t, t)
        kt = kb[pl.ds(start, t), :]
        vt = vb[pl.ds(start, t), :]
        chains = [(hh, slice(r * t, (r + 1) * t), causal if masked else None)
                  for r, masked in subs for hh in range(2)]
        zs = [lax.dot_general(qm[hh, rows], kt, (((1,), (1,)), ((), ())), preferred_element_type=F32) + bias[hh]
              for hh, rows, _ in chains]
        logs = [_sb_logs(z, mask) for z, (_, _, mask) in zip(zs, chains)]
        mass = [_sb_mass(lk, tri) for _, lk in logs]
        pvs = []
        for (hh, rows, mask), (ls, _), r in zip(chains, logs, mass):
            c_old = cc[hh, rows]
            w = jnp.exp(ls + r[:, :t] + c_old)
            if mask is not None:
                w = jnp.where(mask, w, 0.0)
            cc[hh, rows] = c_old + r[:, t:]
            pvs.append(jnp.dot(w.astype(BF16), vt, preferred_element_type=F32))
        for (hh, rows, _), pv in zip(chains, pvs):
            acc[hh, rows] += pv

    for c in range(nq - 1, -1, -1):
        tile(i * nq + c, [(r, r == c) for r in range(c, nq)])

    def body(jj, carry):
        tile(i * nq - 1 - jj, [(r, False) for r in range(nq)])
        return carry

    lax.fori_loop(0, i * nq, body, 0)
    o_ref[...] = jnp.where(first_head, acc[0], acc[1]).astype(o_ref.dtype)


def _sbp_call(proj, bias, tri):
    b, s, _ = proj.shape
    t = KEY_TILE
    nq = _pick_tile(s // t, SBP_SUBTILES)
    tq = nq * t
    assert s % tq == 0
    kern = functools.partial(_sbp_kernel, scale=SB_HEAD_DIM ** -0.5, nq=nq)
    return pl.pallas_call(
        kern,
        grid=(b, SB_WIDTH // LANES, s // tq),
        in_specs=[pl.BlockSpec(memory_space=pltpu.SMEM),
                  pl.BlockSpec((None, tq, LANES), lambda bi, hp, i: (bi, i, C_Q // LANES + hp)),
                  pl.BlockSpec((None, s, LANES), lambda bi, hp, i: (bi, 0, C_K // LANES + hp)),
                  pl.BlockSpec((None, s, LANES), lambda bi, hp, i: (bi, 0, C_V // LANES + hp)),
                  pl.BlockSpec((t, 2 * t), lambda bi, hp, i: (0, 0))],
        out_specs=pl.BlockSpec((None, tq, LANES), lambda bi, hp, i: (bi, i, hp)),
        out_shape=jax.ShapeDtypeStruct((b, s, SB_WIDTH), BF16),
        scratch_shapes=[pltpu.VMEM((s, LANES), BF16), pltpu.VMEM((s, LANES), BF16),
                        pltpu.VMEM((2, tq, LANES), BF16),
                        pltpu.VMEM((2, tq, LANES), F32), pltpu.VMEM((2, tq, LANES), F32)],
        compiler_params=_cparams(("arbitrary", "arbitrary", "arbitrary")),
        name="sb_prompt",
    )(bias, proj, proj, proj, tri)


def _sbs_kernel(pt_ref, qm_ref, bias_ref, kn_ref, vn_ref, tri_ref, *rest, n_pg, n_q):
    del pt_ref
    k_refs = rest[:n_pg]
    v_refs = rest[n_pg:2 * n_pg]
    o_ref, acc, cc = rest[2 * n_pg:]
    t = KEY_TILE
    m = n_q * SB_HEADS
    p = pl.program_id(1)
    qm = qm_ref[...]
    bias = bias_ref[...]
    tri = tri_ref[...]

    def pv(w, vt):
        return lax.dot_general(w.astype(BF16), vt, (((1,), (1,)), ((), ())), preferred_element_type=F32)

    @pl.when(p == 0)
    def _():
        row = lax.broadcasted_iota(jnp.int32, (m, t), 0)
        col = lax.broadcasted_iota(jnp.int32, (m, t), 1)
        z = jnp.dot(qm, kn_ref[...], preferred_element_type=F32) + bias
        w, c_new = _sb_tile(z, tri, jnp.zeros((m, t), F32), col < row // SB_HEADS)
        cc[...] = c_new
        acc[...] = pv(w, vn_ref[...])

    logs = [_sb_logs(jnp.dot(qm, k_refs[r][...].reshape(SB_WIDTH, t).astype(BF16),
                             preferred_element_type=F32) + bias) for r in range(n_pg)]
    mass = [_sb_mass(lk, tri) for _, lk in logs]
    c = cc[...]
    out = acc[...]
    for r in range(n_pg):
        w = jnp.exp(logs[r][0] + mass[r][:, :t] + c)
        c = c + mass[r][:, t:]
        out = out + pv(w, v_refs[r][...].reshape(SB_WIDTH, t).astype(BF16))
    cc[...] = c
    acc[...] = out

    @pl.when(p == pl.num_programs(1) - 1)
    def _():
        a = acc[...]
        row = lax.broadcasted_iota(jnp.int32, (m, SB_WIDTH), 0)
        lane = lax.broadcasted_iota(jnp.int32, (m, SB_WIDTH), 1)
        own = (lane // SB_HEAD_DIM) == (row % SB_HEADS)
        o_ref[...] = jnp.sum(jnp.where(own, a, 0.0).reshape(n_q, SB_HEADS, SB_WIDTH), axis=1)


def _sbs_call(layer, page_table, qm, bias_rows, kn, vn, tri, cache_kt, cache_vt, n_q):
    bs, n_pages = page_table.shape
    n_pg = min(PAGES_PER_STEP, n_pages)
    assert n_pages % n_pg == 0
    m = n_q * SB_HEADS
    t = KEY_TILE
    assert cache_kt.shape[-1] == t

    def page_spec(r):
        def imap(b, p, pt):
            return (layer, pt[b, n_pages - 1 - (p * n_pg + r)], 0, 0, 0)
        return pl.BlockSpec((None, None, SB_HEADS, SB_HEAD_DIM, t), imap)

    grid_spec = pltpu.PrefetchScalarGridSpec(
        num_scalar_prefetch=1,
        grid=(bs, n_pages // n_pg),
        in_specs=[pl.BlockSpec((None, m, SB_WIDTH), lambda b, p, pt: (b, 0, 0)),
                  pl.BlockSpec((m, t), lambda b, p, pt: (0, 0)),
                  pl.BlockSpec((None, SB_WIDTH, t), lambda b, p, pt: (b, 0, 0)),
                  pl.BlockSpec((None, SB_WIDTH, t), lambda b, p, pt: (b, 0, 0)),
                  pl.BlockSpec((t, 2 * t), lambda b, p, pt: (0, 0))]
        + [page_spec(r) for r in range(n_pg)] + [page_spec(r) for r in range(n_pg)],
        out_specs=pl.BlockSpec((None, n_q, SB_WIDTH), lambda b, p, pt: (b, 0, 0)),
        scratch_shapes=[pltpu.VMEM((m, SB_WIDTH), F32), pltpu.VMEM((m, t), F32)],
    )
    return pl.pallas_call(
        functools.partial(_sbs_kernel, n_pg=n_pg, n_q=n_q),
        grid_spec=grid_spec,
        out_shape=jax.ShapeDtypeStruct((bs, n_q, SB_WIDTH), F32),
        compiler_params=_cparams(("arbitrary", "arbitrary")),
        name="sb_sample",
    )(page_table, qm, bias_rows, kn, vn, tri, *([cache_kt] * n_pg), *([cache_vt] * n_pg))


def _causal_conv(xp, w_ref, b_ref, n):
    y = b_ref[...] + w_ref[CONV_WIDTH - 1:CONV_WIDTH, :] * xp[ROW_PAD:]
    for j in range(CONV_WIDTH - 1):
        y = y + w_ref[j:j + 1, :] * pltpu.roll(xp, CONV_WIDTH - 1 - j, 0)[ROW_PAD:]
    return y


def _last_rows(xp, t_valid):
    n = xp.shape[0]
    if t_valid % SUBLANES == 0:
        return xp[t_valid:t_valid + ROW_PAD]
    return pltpu.roll(xp, n - t_valid, 0)[:ROW_PAD]


def _neg_expm1(x):
    p = x * (1.0 + x * (1 / 2 + x * (1 / 6 + x * (1 / 24 + x * (1 / 120 + x * (1 / 720 + x * (1 / 5040)))))))
    return jnp.where(x > -0.25, -p, 1.0 - jnp.exp(x))


def _gelu_tanh(x):
    return 0.5 * x * (1.0 + jnp.tanh(math.sqrt(2.0 / math.pi) * (x + 0.044715 * (x * x * x))))


def _rg_kernel(x_ref, g_ref, conv0_ref, h0_ref, cw_ref, cb_ref, wa_ref, ba_ref, wx_ref, bx_ref, spl_ref,
               y_ref, convn_ref, hn_ref, prev_sc, h_sc, hbuf, *, tt, t_valid, from_start):
    ti = pl.program_id(1)

    @pl.when(ti == 0)
    def _():
        prev_sc[...] = conv0_ref[...]
        h_sc[...] = h0_ref[...]

    x = x_ref[...]
    xp = jnp.concatenate([prev_sc[...], x], axis=0)
    xc = _causal_conv(xp, cw_ref, cb_ref, tt)
    convn_ref[...] = _last_rows(xp, t_valid)
    prev_sc[...] = xp[tt:tt + ROW_PAD]

    xb = xc.astype(BF16)
    r = _sigmoid(jnp.dot(xb, wa_ref[...], preferred_element_type=F32) + ba_ref[...])
    ig = _sigmoid(jnp.dot(xb, wx_ref[...], preferred_element_type=F32) + bx_ref[...])
    log_a = -RG_C * r * spl_ref[...]
    a = jnp.exp(log_a)
    mult = jnp.sqrt(_neg_expm1(2.0 * log_a))
    rowi = lax.broadcasted_iota(jnp.int32, (tt, RG_WIDTH), 0)
    if from_start:
        mult = jnp.where(rowi + ti * tt == 0, 1.0, mult)
    u = xc * ig * mult
    s = 1
    while s < tt:
        keep = rowi >= s
        u = jnp.where(keep, u + a * pltpu.roll(u, s, 0), u)
        a = jnp.where(keep, a * pltpu.roll(a, s, 0), a)
        s *= 2
    h = u + a * h_sc[...]
    hbuf[...] = h
    h_last = hbuf[t_valid - 1:t_valid, :]
    h_sc[...] = h_last
    hn_ref[...] = h_last
    y_ref[...] = (_gelu_tanh(g_ref[...]) * h).astype(y_ref.dtype)


def _rg_call(proj, conv0, h0, cw, cb, wa, ba, wx, bx, spl, *, tt, t_valid, from_start):
    b, t, _ = proj.shape
    assert t % tt == 0 and (t_valid == tt or t == tt)
    w = RG_WIDTH
    vec = pl.BlockSpec((1, w), lambda bi, ti: (0, 0))
    mat = pl.BlockSpec((w, w), lambda bi, ti: (0, 0))
    kern = functools.partial(_rg_kernel, tt=tt, t_valid=t_valid, from_start=from_start)
    return pl.pallas_call(
        kern,
        grid=(b, t // tt),
        in_specs=[pl.BlockSpec((None, tt, w), lambda bi, ti: (bi, ti, C_RGX // w)),
                  pl.BlockSpec((None, tt, w), lambda bi, ti: (bi, ti, C_RGG // w)),
                  pl.BlockSpec((None, ROW_PAD, w), lambda bi, ti: (bi, 0, 0)),
                  pl.BlockSpec((None, 1, w), lambda bi, ti: (bi, 0, 0)),
                  pl.BlockSpec((CONV_WIDTH, w), lambda bi, ti: (0, 0)), vec, mat, vec, mat, vec, vec],
        out_specs=[pl.BlockSpec((None, tt, w), lambda bi, ti: (bi, ti, 0)),
                   pl.BlockSpec((None, ROW_PAD, w), lambda bi, ti: (bi, 0, 0)),
                   pl.BlockSpec((None, 1, w), lambda bi, ti: (bi, 0, 0))],
        out_shape=[jax.ShapeDtypeStruct((b, t, w), BF16),
                   jax.ShapeDtypeStruct((b, ROW_PAD, w), F32),
                   jax.ShapeDtypeStruct((b, 1, w), F32)],
        scratch_shapes=[pltpu.VMEM((ROW_PAD, w), F32), pltpu.VMEM((1, w), F32), pltpu.VMEM((tt, w), F32)],
        compiler_params=_cparams(("arbitrary", "arbitrary")),
        name="rg_lru",
    )(proj, proj, conv0, h0, cw, cb, wa, ba, wx, bx, spl)


def _pad_rows(x, n):
    if x.shape[0] == n:
        return x
    return jnp.concatenate([x, jnp.zeros((n - x.shape[0], x.shape[1]), x.dtype)], axis=0)


def _ssd_kernel(z_ref, xbc_ref, dt_ref, conv0_ref, h0_ref, cw_ref, cb_ref, dtb_ref, a_ref, d_ref, nw_ref,
                y_ref, convn_ref, hn_ref, prev_sc, h_sc, *, l_in, t_valid):
    L = SSD_CHUNK
    N = SSD_STATE
    P2 = 2 * SSD_HEAD_DIM
    ci = pl.program_id(1)

    @pl.when(ci == 0)
    def _():
        prev_sc[...] = conv0_ref[...]
        h_sc[...] = h0_ref[...]

    xbc = _pad_rows(xbc_ref[...], L)
    xp = jnp.concatenate([prev_sc[...], xbc], axis=0)
    xc = _causal_conv(xp, cw_ref, cb_ref, L)
    convn_ref[...] = _last_rows(xp, t_valid)
    prev_sc[...] = xp[L:L + ROW_PAD]
    xc = xc * _sigmoid(xc)
    xs = xc[:, :SSD_INNER]
    bm = xc[:, SSD_INNER:SSD_INNER + SSD_GROUPS * N].astype(BF16)
    cm = xc[:, SSD_INNER + SSD_GROUPS * N:].astype(BF16)

    row = lax.broadcasted_iota(jnp.int32, (L, LANES), 0)
    lane = lax.broadcasted_iota(jnp.int32, (L, LANES), 1)
    dtr = _pad_rows(dt_ref[...], L) + dtb_ref[...]
    dt = jnp.maximum(dtr, 0.0) + jnp.log1p(jnp.exp(-jnp.abs(dtr)))
    dt = jnp.where((lane < SSD_HEADS) & (row < t_valid), dt, 0.0)
    cs = dt * a_ref[...]
    s = 1
    while s < L:
        cs = jnp.where(row >= s, cs + pltpu.roll(cs, s, 0), cs)
        s *= 2
    ecs = jnp.exp(cs)
    cs_t = cs.T
    dt_t = dt.T
    xs_t = xs.T
    tril = row >= lane
    first_head = lane < SSD_HEAD_DIM
    first_rows = row < SSD_HEAD_DIM

    y_pairs = []
    for hp in range(SSD_HEADS // 2):
        g = (2 * hp) // (SSD_HEADS // SSD_GROUPS)
        bm_g = bm[:, g * N:(g + 1) * N]
        cm_g = cm[:, g * N:(g + 1) * N]
        cb = lax.dot_general(cm_g, bm_g, (((1,), (1,)), ((), ())), preferred_element_type=F32)
        x2 = xs[:, hp * P2:(hp + 1) * P2]
        x2b = x2.astype(BF16)
        yd, wrow, cdec, eoff = [], [], [], []
        for hh in range(2):
            h = 2 * hp + hh
            seg = cs[:, h:h + 1] - cs_t[h:h + 1, :]
            decay = jnp.exp(jnp.where(tril, seg, -jnp.inf))
            wts = cb * decay * dt_t[h:h + 1, :]
            yd.append(jnp.dot(wts.astype(BF16), x2b, preferred_element_type=F32))
            last = cs_t[h:h + 1, L - 1:L]
            wrow.append(jnp.exp(last - cs_t[h:h + 1, :]) * dt_t[h:h + 1, :])
            cdec.append(jnp.exp(last))
            eoff.append(ecs[:, h:h + 1])
        h_pair = h_sc[hp * P2:(hp + 1) * P2, :]
        y_off = lax.dot_general(cm_g, h_pair.astype(BF16), (((1,), (1,)), ((), ())),
                                preferred_element_type=F32)
        y_off = y_off * jnp.where(first_head, eoff[0], eoff[1])
        y_pairs.append(jnp.where(first_head, yd[0], yd[1]) + y_off + d_ref[:, hp * P2:(hp + 1) * P2] * x2)
        xw = xs_t[hp * P2:(hp + 1) * P2, :] * jnp.where(first_rows, wrow[0], wrow[1])
        st = jnp.dot(xw.astype(BF16), bm_g, preferred_element_type=F32)
        h_sc[hp * P2:(hp + 1) * P2, :] = jnp.where(first_rows, cdec[0], cdec[1]) * h_pair + st

    hn_ref[...] = h_sc[...]
    y = jnp.concatenate(y_pairs, axis=1)[:l_in]
    zz = z_ref[...]
    u = y * (zz * _sigmoid(zz))
    gw = SSD_INNER // SSD_GROUPS
    u = jnp.concatenate([_rms(u[:, g * gw:(g + 1) * gw]) for g in range(SSD_GROUPS)], axis=1)
    y_ref[...] = (u * nw_ref[...]).astype(y_ref.dtype)


def _ssd_call(proj, conv0, h0, cw, cb, dtb, a_row, d_lane, nw, *, l_in, t_valid):
    b, t, _ = proj.shape
    assert t % l_in == 0 and (l_in == SSD_CHUNK or t == l_in)
    c = SSD_CONV_DIM
    hp_rows = SSD_HEADS * SSD_HEAD_DIM
    kern = functools.partial(_ssd_kernel, l_in=l_in, t_valid=t_valid)
    return pl.pallas_call(
        kern,
        grid=(b, t // l_in),
        in_specs=[pl.BlockSpec((None, l_in, SSD_INNER), lambda bi, ci: (bi, ci, C_Z // SSD_INNER)),
                  pl.BlockSpec((None, l_in, c), lambda bi, ci: (bi, ci, C_XBC // c)),
                  pl.BlockSpec((None, l_in, LANES), lambda bi, ci: (bi, ci, C_DT // LANES)),
                  pl.BlockSpec((None, ROW_PAD, c), lambda bi, ci: (bi, 0, 0)),
                  pl.BlockSpec((None, hp_rows, SSD_STATE), lambda bi, ci: (bi, 0, 0)),
                  pl.BlockSpec((CONV_WIDTH, c), lambda bi, ci: (0, 0)),
                  pl.BlockSpec((1, c), lambda bi, ci: (0, 0)),
                  pl.BlockSpec((1, LANES), lambda bi, ci: (0, 0)),
                  pl.BlockSpec((1, LANES), lambda bi, ci: (0, 0)),
                  pl.BlockSpec((1, SSD_INNER), lambda bi, ci: (0, 0)),
                  pl.BlockSpec((1, SSD_INNER), lambda bi, ci: (0, 0))],
        out_specs=[pl.BlockSpec((None, l_in, SSD_INNER), lambda bi, ci: (bi, ci, 0)),
                   pl.BlockSpec((None, ROW_PAD, c), lambda bi, ci: (bi, 0, 0)),
                   pl.BlockSpec((None, hp_rows, SSD_STATE), lambda bi, ci: (bi, 0, 0))],
        out_shape=[jax.ShapeDtypeStruct((b, t, SSD_INNER), BF16),
                   jax.ShapeDtypeStruct((b, ROW_PAD, c), F32),
                   jax.ShapeDtypeStruct((b, hp_rows, SSD_STATE), F32)],
        scratch_shapes=[pltpu.VMEM((ROW_PAD, c), F32), pltpu.VMEM((hp_rows, SSD_STATE), F32)],
        compiler_params=_cparams(("arbitrary", "arbitrary")),
        name="ssd",
    )(proj, proj, proj, conv0, h0, cw, cb, dtb, a_row, d_lane, nw)


def _merge_kernel(ya_ref, yb_ref, yc_ref, g0_ref, g1_ref, g2_ref, x_ref, gt_ref, sc_ref, sh_ref,
                  wa_ref, wb_ref, wc_ref, wo_ref, nf_ref, wr_ref, br_ref, x1_ref, h2_ref, lg_ref):
    def branch(y_ref, w_ref, g_ref):
        return _sigmoid(g_ref[...]) * jnp.dot(y_ref[...], w_ref[...], preferred_element_type=F32)

    m = branch(ya_ref, wa_ref, g0_ref) + branch(yb_ref, wb_ref, g1_ref) + branch(yc_ref, wc_ref, g2_ref)
    o = jnp.dot(m.astype(BF16), wo_ref[...], preferred_element_type=F32)
    x1 = x_ref[...] + gt_ref[...] * o
    x1_ref[...] = x1
    h2 = _rms(x1) * nf_ref[...] * (1.0 + sc_ref[...]) + sh_ref[...]
    hi = h2.astype(BF16)
    h2_ref[...] = hi
    lo = (h2 - hi.astype(F32)).astype(BF16)
    lg_ref[...] = (jnp.dot(hi, wr_ref[0], preferred_element_type=F32)
                   + jnp.dot(lo, wr_ref[0], preferred_element_type=F32)
                   + jnp.dot(hi, wr_ref[1], preferred_element_type=F32) + br_ref[...])


def _merge_call(ya, yb, yc, proj, x, gt, sc, sh, wa, wb, wc, wo, nf, wr, br, *, per_row, rows_per_group, tm):
    r = x.shape[0]
    assert r % tm == 0 and (per_row or rows_per_group % tm == 0)
    d = D_MODEL
    ms = _mod_spec(per_row, tm, rows_per_group)
    yspec = pl.BlockSpec((tm, SB_WIDTH), lambda i: (i, 0))
    gspec = [pl.BlockSpec((tm, d), lambda i, k=k: (i, C_BRG // d + k)) for k in range(N_BRANCH)]
    wspec = pl.BlockSpec((SB_WIDTH, d), lambda i: (0, 0))
    return pl.pallas_call(
        _merge_kernel,
        grid=(r // tm,),
        in_specs=[yspec, yspec, yspec, *gspec, pl.BlockSpec((tm, d), lambda i: (i, 0)), ms, ms, ms,
                  wspec, wspec, wspec, pl.BlockSpec((d, d), lambda i: (0, 0)),
                  pl.BlockSpec((1, d), lambda i: (0, 0)),
                  pl.BlockSpec((2, d, LANES), lambda i: (0, 0, 0)), pl.BlockSpec((1, LANES), lambda i: (0, 0))],
        out_specs=[pl.BlockSpec((tm, d), lambda i: (i, 0)), pl.BlockSpec((tm, d), lambda i: (i, 0)),
                   pl.BlockSpec((tm, LANES), lambda i: (i, 0))],
        out_shape=[jax.ShapeDtypeStruct((r, d), F32), jax.ShapeDtypeStruct((r, d), BF16),
                   jax.ShapeDtypeStruct((r, LANES), F32)],
        compiler_params=_cparams(("arbitrary",)),
        name="merge",
    )(ya, yb, yc, proj, proj, proj, x, gt, sc, sh, wa, wb, wc, wo, nf, wr, br)


def _moe_kernel(be_ref, xs_ref, *rest):
    ncb = D_MODEL // LANES
    wu_refs = rest[:ncb]
    bg_ref, bl_ref, wd_ref, bd_ref, o_ref, wg_sc, wl_sc, wd_sc = rest[ncb:]
    i = pl.program_id(0)

    @pl.when((i == 0) | (be_ref[i] != be_ref[jnp.maximum(i - 1, 0)]))
    def _():
        rc = MOE_BM
        for c in range(D_FF // rc):
            rows = slice(c * rc, (c + 1) * rc)
            for cb in range(ncb):
                cols = slice(cb * LANES, (cb + 1) * LANES)
                wg_sc[rows, cols] = wu_refs[cb][pl.ds(2 * c * rc, rc, stride=2), :].astype(BF16)
                wl_sc[rows, cols] = wu_refs[cb][pl.ds(2 * c * rc + 1, rc, stride=2), :].astype(BF16)
            wd_sc[rows, :] = wd_ref[rows, :].astype(BF16)

    nt = (((1,), (1,)), ((), ()))
    n_active = be_ref[pl.num_programs(0)]

    @pl.when(i < n_active)
    def _():
        x = xs_ref[...]
        glu = lax.dot_general(x, wg_sc[...], nt, preferred_element_type=F32) + bg_ref[...]
        lin = lax.dot_general(x, wl_sc[...], nt, preferred_element_type=F32) + bl_ref[...]
        glu = jnp.minimum(glu, SWIGLU_LIMIT)
        lin = jnp.clip(lin, -SWIGLU_LIMIT, SWIGLU_LIMIT)
        act = glu * _sigmoid(SWIGLU_ALPHA * glu) * (lin + 1.0)
        y = jnp.dot(act.astype(BF16), wd_sc[...], preferred_element_type=F32) + bd_ref[...]
        o_ref[...] = y.astype(o_ref.dtype)

    @pl.when(i >= n_active)
    def _():
        o_ref[...] = jnp.zeros_like(o_ref)


def _moe_call(layer, block_exp, xs, wu_t, bg, bl, wd, bd):
    r = xs.shape[0]
    d = D_MODEL
    assert D_FF % MOE_BM == 0
    bspec = pl.BlockSpec((None, None, 1, d), lambda i, be: (layer, be[i], 0, 0))
    ncb = d // LANES
    wu_specs = [pl.BlockSpec((None, None, 2 * D_FF, LANES), lambda i, be, cb=cb: (layer, be[i], 0, cb))
                for cb in range(ncb)]
    grid_spec = pltpu.PrefetchScalarGridSpec(
        num_scalar_prefetch=1,
        grid=(r // MOE_BM,),
        in_specs=[pl.BlockSpec((MOE_BM, d), lambda i, be: (i, 0)), *wu_specs, bspec, bspec,
                  pl.BlockSpec((None, None, D_FF, d), lambda i, be: (layer, be[i], 0, 0)), bspec],
        out_specs=pl.BlockSpec((MOE_BM, d), lambda i, be: (i, 0)),
        scratch_shapes=[pltpu.VMEM((D_FF, d), BF16), pltpu.VMEM((D_FF, d), BF16), pltpu.VMEM((D_FF, d), BF16)],
    )
    return pl.pallas_call(
        _moe_kernel,
        grid_spec=grid_spec,
        out_shape=jax.ShapeDtypeStruct((r, d), BF16),
        compiler_params=_cparams(("arbitrary",), MOE_VMEM_LIMIT),
        name="moe",
    )(block_exp, xs, *([wu_t] * ncb), bg, bl, wd, bd)


def _route(logits, bm):
    n_tok = logits.shape[0]
    n_slot = n_tok * TOP_K
    n_blocks = -(-(n_slot + N_EXPERTS * (bm - 1)) // bm)
    n_rows = n_blocks * bm
    top_val, top_idx = lax.top_k(logits, TOP_K)
    gate = jax.nn.softmax(top_val, axis=-1)
    e_flat = top_idx.reshape(n_slot).astype(jnp.int32)
    order = jnp.argsort(e_flat, stable=True).astype(jnp.int32)
    rank = jnp.argsort(order).astype(jnp.int32)
    experts = jnp.arange(N_EXPERTS, dtype=jnp.int32)
    e_lanes = jnp.pad(e_flat, (0, (-n_slot) % LANES), constant_values=-1).reshape(-1, LANES)
    counts = jnp.sum((e_lanes[None] == experts[:, None, None]).astype(jnp.int32), axis=(1, 2))
    padded = (counts + bm - 1) // bm * bm
    start = jnp.cumsum(counts) - counts
    pend = jnp.cumsum(padded)
    shift = (pend - padded) - start
    block_row = jnp.arange(n_blocks, dtype=jnp.int32) * bm
    block_exp = jnp.minimum(jnp.sum((pend[None, :] <= block_row[:, None]).astype(jnp.int32), axis=1),
                            N_EXPERTS - 1)
    row = block_row[:, None] + jnp.arange(bm, dtype=jnp.int32)[None, :]
    valid = (row - (pend - padded)[block_exp][:, None]) < counts[block_exp][:, None]
    row_slot = order[jnp.clip(row - shift[block_exp][:, None], 0, n_slot - 1).reshape(n_rows)]
    row_tok = jnp.where(valid, (row_slot // TOP_K).reshape(n_blocks, bm), row % n_tok).reshape(n_rows)
    slot_row = (rank + shift[e_flat]).reshape(n_tok, TOP_K)
    block_plan = jnp.concatenate([block_exp, pend[N_EXPERTS - 1:] // bm]).astype(jnp.int32)
    return block_plan, row_tok, gate, slot_row.T.reshape(n_slot)


def _resid_kernel(x_ref, y_ref, gate_ref, gt_ref, nw_ref, o_ref, *, final):
    gate = gate_ref[...]
    f = gate[:, 0:1] * y_ref[0]
    for k in range(1, TOP_K):
        f = f + gate[:, k:k + 1] * y_ref[k]
    x = x_ref[...] + gt_ref[...] * f
    if final:
        x = _rms(x) * nw_ref[...]
    o_ref[...] = x


def _resid_call(x, ysel, gate, gt, nw, *, row_off, final, per_row, rows_per_group, tm):
    r = x.shape[0]
    d = D_MODEL
    assert r % tm == 0 and row_off % tm == 0
    off = row_off // tm
    row = pl.BlockSpec((tm, d), lambda i: (i, 0))
    return pl.pallas_call(
        functools.partial(_resid_kernel, final=final),
        grid=(r // tm,),
        in_specs=[row, pl.BlockSpec((TOP_K, tm, d), lambda i: (0, i + off, 0)),
                  pl.BlockSpec((tm, TOP_K), lambda i: (i + off, 0)),
                  _mod_spec(per_row, tm, rows_per_group), pl.BlockSpec((1, d), lambda i: (0, 0))],
        out_specs=row,
        out_shape=jax.ShapeDtypeStruct((r, d), F32),
        compiler_params=_cparams(("arbitrary",)),
        name="resid",
    )(x, ysel, gate, gt, nw)


def _block_diag(w):
    depth, nb, k, _ = w.shape
    eye = jnp.eye(nb, dtype=w.dtype)
    return jnp.einsum('lnij,nm->lnimj', w, eye).reshape(depth, nb * k, nb * k)


def _pick_tile(n, pref):
    t = min(pref, n)
    while n % t:
        t //= 2
    return t


def kernel(x_prompt, x_sample, cache_k, cache_v, state_rg_conv, state_rg_h, state_ssd_conv, state_ssd, page_table, c_prompt, c_sample, w_mod, b_mod, norm_mix, norm_ffn, norm_final, w_in, sb_bias, rg_conv_w, rg_conv_b, rg_w_a, rg_b_a, rg_w_x, rg_b_x, rg_lam, ssd_conv_w, ssd_conv_b, ssd_dt_bias, ssd_a_log, ssd_d, ssd_norm, w_branch, w_out, w_router, b_router, w_up, b_up, w_down, b_down):
    depth = w_in.shape[0]
    b, s, d = x_prompt.shape
    bs, ts, _ = x_sample.shape
    tp = ROW_PAD
    n_p, n_s = b * s, bs * tp
    past_len = page_table.shape[1] * cache_k.shape[2]

    w_in_p = jnp.concatenate(
        [w_in[:, :, :4096], w_in[:, :, 4104:], w_in[:, :, 4096:4104],
         jnp.zeros((depth, d, NP_COLS - w_in.shape[2]), w_in.dtype)], axis=2).astype(BF16)
    wb = w_branch.astype(BF16)
    wb_a, wb_b, wb_c = wb[:, :SB_WIDTH], wb[:, SB_WIDTH:SB_WIDTH + RG_WIDTH], wb[:, SB_WIDTH + RG_WIDTH:]
    wo = w_out.astype(BF16)
    wr_f = jnp.pad(w_router, ((0, 0), (0, 0), (0, LANES - N_EXPERTS)))
    wr_hi = wr_f.astype(BF16)
    wr = jnp.stack([wr_hi, (wr_f - wr_hi.astype(F32)).astype(BF16)], axis=1)
    br = jnp.pad(b_router, ((0, 0), (0, LANES - N_EXPERTS))).reshape(depth, 1, LANES)
    wu_t = jnp.swapaxes(w_up, 2, 3)
    bg = b_up[..., 0::2].reshape(depth, N_EXPERTS, 1, D_FF)
    bl = b_up[..., 1::2].reshape(depth, N_EXPERTS, 1, D_FF)
    bd = b_down.reshape(depth, N_EXPERTS, 1, d)
    rg_wa = _block_diag(rg_w_a).astype(BF16)
    rg_wx = _block_diag(rg_w_x).astype(BF16)
    rg_spl = jax.nn.softplus(-rg_lam)
    ssd_a = jnp.pad(-jnp.exp(ssd_a_log), ((0, 0), (0, LANES - SSD_HEADS)))
    ssd_dtb = jnp.pad(ssd_dt_bias, ((0, 0), (0, LANES - SSD_HEADS)))
    ssd_d_lane = jnp.repeat(ssd_d, SSD_HEAD_DIM, axis=1)
    tri = _tri_matrix(KEY_TILE)
    cache_kt = jnp.transpose(cache_k, (0, 1, 3, 4, 2))
    cache_vt = jnp.transpose(cache_v, (0, 1, 3, 4, 2))

    mod = _mod_call(jnp.concatenate([c_prompt, c_sample], axis=0), w_mod, b_mod)

    def row2(v):
        return v.reshape(1, -1)

    def pad_prev(st):
        return jnp.pad(st, ((0, 0), (tp - (CONV_WIDTH - 1), 0), (0, 0)))

    xp = x_prompt.reshape(n_p, d)
    xq = jnp.pad(x_sample, ((0, 0), (0, tp - ts), (0, 0))).reshape(n_s, d)
    tm_in = _pick_tile(s, 1024)
    tm_mg = _pick_tile(s, 256)
    tt_rg = _pick_tile(s, 256)
    tm_rs = _pick_tile(s, 512)
    outs_p, outs_s = [], []
    for l in range(depth):
        mp = mod[l, :b].reshape(b, 1, 6, d)
        mq = jnp.repeat(mod[l, b:], tp, axis=0).reshape(n_s, 6, d)
        sh1p, sc1p, gt1p, sh2p, sc2p, gt2p = (mp[:, :, k] for k in range(6))
        sh1q, sc1q, gt1q, sh2q, sc2q, gt2q = (mq[:, k] for k in range(6))
        bias_l = sb_bias[l]

        proj_p = _inproj_call(xp, sc1p, sh1p, row2(norm_mix[l]), w_in_p[l],
                              per_row=False, rows_per_group=s, tm=tm_in).reshape(b, s, NP_COLS)
        ya_p = _sbp_call(proj_p, bias_l, tri)
        yb_p, rgc_p, rgh_p = _rg_call(
            proj_p, jnp.zeros((b, tp, RG_WIDTH), F32), jnp.zeros((b, 1, RG_WIDTH), F32),
            rg_conv_w[l], row2(rg_conv_b[l]), rg_wa[l], row2(rg_b_a[l]), rg_wx[l], row2(rg_b_x[l]),
            row2(rg_spl[l]), tt=tt_rg, t_valid=tt_rg, from_start=True)
        yc_p, ssc_p, ss_p = _ssd_call(
            proj_p, jnp.zeros((b, tp, SSD_CONV_DIM), F32),
            jnp.zeros((b, SSD_HEADS * SSD_HEAD_DIM, SSD_STATE), F32),
            ssd_conv_w[l], row2(ssd_conv_b[l]), row2(ssd_dtb[l]), row2(ssd_a[l]), row2(ssd_d_lane[l]),
            row2(ssd_norm[l]), l_in=SSD_CHUNK, t_valid=SSD_CHUNK)
        x1p, h2p, lgp = _merge_call(
            ya_p.reshape(n_p, -1), yb_p.reshape(n_p, -1), yc_p.reshape(n_p, -1), proj_p.reshape(n_p, NP_COLS),
            xp, gt1p, sc2p, sh2p, wb_a[l], wb_b[l], wb_c[l], wo[l], row2(norm_ffn[l]), wr[l], br[l],
            per_row=False, rows_per_group=s, tm=tm_mg)
        outs_p.append((proj_p[:, :, C_K:C_K + SB_WIDTH], proj_p[:, :, C_V:C_V + SB_WIDTH],
                       rgc_p[:, tp - 3:], rgh_p[:, 0], ssc_p[:, tp - 3:], ss_p))

        proj_q = _inproj_call(xq, sc1q, sh1q, row2(norm_mix[l]), w_in_p[l],
                              per_row=True, rows_per_group=tp, tm=n_s).reshape(bs, tp, NP_COLS)
        q_new = proj_q[:, :ts, C_Q:C_Q + SB_WIDTH] * (SB_HEAD_DIM ** -0.5)
        k_new = proj_q[:, :ts, C_K:C_K + SB_WIDTH]
        v_new = proj_q[:, :ts, C_V:C_V + SB_WIDTH]
        head_of_col = jnp.arange(SB_WIDTH) // SB_HEAD_DIM
        own = head_of_col[None, :] == jnp.arange(SB_HEADS)[:, None]
        qm = jnp.where(own[None, None], q_new[:, :, None, :], 0.0).reshape(bs, ts * SB_HEADS, SB_WIDTH)
        bias_rows = jnp.broadcast_to(jnp.tile(bias_l, ts)[:, None], (ts * SB_HEADS, KEY_TILE))
        kn = jnp.pad(jnp.swapaxes(k_new, 1, 2), ((0, 0), (0, 0), (0, KEY_TILE - ts))).astype(BF16)
        vn = jnp.pad(jnp.swapaxes(v_new, 1, 2), ((0, 0), (0, 0), (0, KEY_TILE - ts))).astype(BF16)
        ya_q = _sbs_call(l, page_table, qm.astype(BF16), bias_rows, kn, vn, tri, cache_kt, cache_vt, ts)
        ya_q = jnp.pad(ya_q, ((0, 0), (0, tp - ts), (0, 0))).astype(BF16)
        yb_q, rgc_q, rgh_q = _rg_call(
            proj_q, pad_prev(state_rg_conv[l]), state_rg_h[l][:, None, :],
            rg_conv_w[l], row2(rg_conv_b[l]), rg_wa[l], row2(rg_b_a[l]), rg_wx[l], row2(rg_b_x[l]),
            row2(rg_spl[l]), tt=tp, t_valid=ts, from_start=(past_len == 0))
        yc_q, ssc_q, ss_q = _ssd_call(
            proj_q, pad_prev(state_ssd_conv[l]),
            state_ssd[l].reshape(bs, SSD_HEADS * SSD_HEAD_DIM, SSD_STATE),
            ssd_conv_w[l], row2(ssd_conv_b[l]), row2(ssd_dtb[l]), row2(ssd_a[l]), row2(ssd_d_lane[l]),
            row2(ssd_norm[l]), l_in=tp, t_valid=ts)
        x1q, h2q, lgq = _merge_call(
            ya_q.reshape(n_s, -1), yb_q.reshape(n_s, -1), yc_q.reshape(n_s, -1), proj_q.reshape(n_s, NP_COLS),
            xq, gt1q, sc2q, sh2q, wb_a[l], wb_b[l], wb_c[l], wo[l], row2(norm_ffn[l]), wr[l], br[l],
            per_row=True, rows_per_group=tp, tm=n_s)
        outs_s.append((k_new, v_new, rgc_q[:, tp - 3:], rgh_q[:, 0], ssc_q[:, tp - 3:], ss_q))

        h2 = jnp.concatenate([h2p, h2q], axis=0)
        logits = jnp.concatenate([lgp, lgq], axis=0)[:, :N_EXPERTS]
        block_exp, row_tok, gate, slot_row = _route(logits, MOE_BM)
        ys = _moe_call(l, block_exp, h2[row_tok], wu_t, bg, bl, w_down, bd)
        ysel = ys[slot_row].reshape(TOP_K, n_p + n_s, d)
        final = l == depth - 1
        xp = _resid_call(x1p, ysel, gate, gt2p, row2(norm_final), row_off=0, final=final,
                         per_row=False, rows_per_group=s, tm=tm_rs)
        xq = _resid_call(x1q, ysel, gate, gt2q, row2(norm_final), row_off=n_p, final=final,
                         per_row=True, rows_per_group=tp, tm=n_s)

    def stack(outs, i, shape):
        return jnp.stack([o[i] for o in outs]).reshape(shape)

    y_prompt = xp.reshape(b, s, d)
    y_sample = xq.reshape(bs, tp, d)[:, :ts]
    hs = (SSD_HEADS, SSD_HEAD_DIM, SSD_STATE)
    return (y_prompt, y_sample,
            stack(outs_p, 0, (depth, b, s, SB_HEADS, SB_HEAD_DIM)),
            stack(outs_p, 1, (depth, b, s, SB_HEADS, SB_HEAD_DIM)),
            stack(outs_p, 2, (depth, b, CONV_WIDTH - 1, RG_WIDTH)), stack(outs_p, 3, (depth, b, RG_WIDTH)),
            stack(outs_p, 4, (depth, b, CONV_WIDTH - 1, SSD_CONV_DIM)), stack(outs_p, 5, (depth, b) + hs),
            stack(outs_s, 0, (depth, bs, ts, SB_HEADS, SB_HEAD_DIM)),
            stack(outs_s, 1, (depth, bs, ts, SB_HEADS, SB_HEAD_DIM)),
            stack(outs_s, 2, (depth, bs, CONV_WIDTH - 1, RG_WIDTH)), stack(outs_s, 3, (depth, bs, RG_WIDTH)),
            stack(outs_s, 4, (depth, bs, CONV_WIDTH - 1, SSD_CONV_DIM)), stack(outs_s, 5, (depth, bs) + hs))
```
